```python
import jax, jax.numpy as jnp
from jax import lax
import numpy as np

D_MODEL = 1024
BATCH = 4
SEQ = 8192
DEPTH = 1
DEC_BATCH = 32
DEC_SEQ = 8
PAST_LEN = 16384
PAGE_SIZE = 128

HEAD_DIM = 64
ROPE_DIM = HEAD_DIM // 4
ROPE_THETA = 500000.0
QK_SCALE = HEAD_DIM ** -0.5
EPS = 1e-6
NEG_INF = -1e30
Q_BLOCK = 128
A_HEADS = 8
A_GROUPS = 2
CMP_STRIDE = 16
CMP_LEN = 2 * CMP_STRIDE
SLC_BLOCK = 64
SLC_TOP = 16
FORCE_BONUS = 1e4
WINDOW = 512
B_HEADS = 8
B_KV_HEADS = 2
IDX_HEADS = 4
IDX_DIM = 64
IDX_ROPE_DIM = IDX_DIM // 4
IDX_SCALE = (IDX_HEADS * IDX_DIM) ** -0.5
DSA_TOPK_MAX = 256
PEER_HEADS = 8
PEER_NKEYS = 128
PEER_EXPERTS = PEER_NKEYS * PEER_NKEYS
PEER_KEY_DIM = 256
PEER_TOPK = 16
PEER_CHUNK = 128

IN_SECTIONS = (
    ('a_q', A_HEADS * HEAD_DIM), ('a_k_cmp', A_GROUPS * HEAD_DIM), ('a_v_cmp', A_GROUPS * HEAD_DIM),
    ('a_k_slc', A_GROUPS * HEAD_DIM), ('a_v_slc', A_GROUPS * HEAD_DIM),
    ('a_k_win', A_GROUPS * HEAD_DIM), ('a_v_win', A_GROUPS * HEAD_DIM), ('a_gate', 3 * A_HEADS),
    ('b_q', B_HEADS * HEAD_DIM), ('b_k', B_KV_HEADS * HEAD_DIM), ('b_v', B_KV_HEADS * HEAD_DIM),
    ('i_q', IDX_HEADS * IDX_DIM), ('i_w', IDX_HEADS), ('i_k', IDX_DIM), ('merge', 2 * D_MODEL))
D_IN = sum(n for _, n in IN_SECTIONS)
QUERY_KEYS = ('a_q', 'a_gate', 'b_q', 'i_q', 'i_w')

kernel_name = 'nsa_dsa_peer_hybrid_step'


def rmsnorm(x, g):
    xf = x.astype(jnp.float32)
    y = xf * lax.rsqrt(jnp.mean(xf * xf, axis=-1, keepdims=True) + EPS)
    return (y * g.astype(jnp.float32)).astype(x.dtype)


def partial_rope(x, pos, rot_dim):
    half = rot_dim // 2
    inv = jnp.power(ROPE_THETA, -jnp.arange(half, dtype=jnp.float32) / half)
    ang = pos.astype(jnp.float32)[:, None] * inv[None, :]
    cos = jnp.cos(ang)[:, None, :].astype(x.dtype)
    sin = jnp.sin(ang)[:, None, :].astype(x.dtype)
    x1, x2, rest = x[..., :half], x[..., half:rot_dim], x[..., rot_dim:]
    return jnp.concatenate([x1 * cos - x2 * sin, x2 * cos + x1 * sin, rest], axis=-1)


def split_in(z):
    offs = np.cumsum([n for _, n in IN_SECTIONS])[:-1].tolist()
    parts = jnp.split(z, offs, axis=-1)
    return {name: p for (name, _), p in zip(IN_SECTIONS, parts)}


def project(xn, pos, w_in, g_qa, g_ka_cmp, g_ka_slc, g_ka_win, g_qb, g_kb, g_kidx):
    B, T, _ = xn.shape
    p = split_in(jnp.einsum('btd,de->bte', xn, w_in))

    def qk(name, n, g):
        return partial_rope(rmsnorm(p[name].reshape(B, T, n, HEAD_DIM), g), pos, ROPE_DIM)

    def vv(name, n):
        return p[name].reshape(B, T, n, HEAD_DIM)

    i_k = rmsnorm(p['i_k'], g_kidx)[:, :, None, :]
    return {
        'a_q': qk('a_q', A_HEADS, g_qa),
        'a_k_cmp': qk('a_k_cmp', A_GROUPS, g_ka_cmp), 'a_v_cmp': vv('a_v_cmp', A_GROUPS),
        'a_k_slc': qk('a_k_slc', A_GROUPS, g_ka_slc), 'a_v_slc': vv('a_v_slc', A_GROUPS),
        'a_k_win': qk('a_k_win', A_GROUPS, g_ka_win), 'a_v_win': vv('a_v_win', A_GROUPS),
        'a_gate': jax.nn.sigmoid(p['a_gate'].astype(jnp.float32)).astype(xn.dtype).reshape(B, T, 3, A_HEADS),
        'b_q': qk('b_q', B_HEADS, g_qb),
        'b_k': qk('b_k', B_KV_HEADS, g_kb), 'b_v': vv('b_v', B_KV_HEADS),
        'i_q': partial_rope(p['i_q'].reshape(B, T, IDX_HEADS, IDX_DIM), pos, IDX_ROPE_DIM),
        'i_w': p['i_w'] * IDX_SCALE,
        'i_k': partial_rope(i_k, pos, IDX_ROPE_DIM)[:, :, 0, :],
        'merge': p['merge'],
    }


def compress(k, w_c, pe):
    B, L, G, d = k.shape
    n_chunks = L // CMP_STRIDE
    ch = k[:, :n_chunks * CMP_STRIDE].reshape(B, n_chunks, CMP_STRIDE, G, d)
    blocks = jnp.concatenate([ch[:, :-1], ch[:, 1:]], axis=2)
    return jnp.einsum('bnjgd,jde->bnge', blocks + pe[None, None, :, None, :], w_c)


def cmp_to_slc(n_cmp, n_slc):
    i = np.arange(n_cmp)[:, None]
    j = np.arange(n_slc)[None, :]
    start = i * CMP_STRIDE
    ov = np.minimum(start + CMP_LEN, (j + 1) * SLC_BLOCK) - np.maximum(start, j * SLC_BLOCK)
    return jnp.asarray(np.clip(ov, 0, None) / CMP_LEN, dtype=jnp.float32)


def masked_attn(q, k, v, mask):
    s = jnp.einsum('btgrd,bmgd->btgrm', q, k).astype(jnp.float32) * QK_SCALE
    s = jnp.where(mask[None, :, None, None, :], s, NEG_INF)
    p = jax.nn.softmax(s, axis=-1)
    p = p * jnp.any(mask, axis=-1).astype(jnp.float32)[None, :, None, None, None]
    return jnp.einsum('btgrm,bmgd->btgrd', p.astype(v.dtype), v), p


def gathered_attn(q, k, v, valid):
    s = jnp.einsum('btgrd,btgmd->btgrm', q, k).astype(jnp.float32) * QK_SCALE
    s = jnp.where(valid[:, :, :, None, :], s, NEG_INF)
    p = jax.nn.softmax(s, axis=-1)
    return jnp.einsum('btgrm,btgmd->btgrd', p.astype(v.dtype), v)


def gather_dense(k, idx):
    b = jnp.arange(k.shape[0])[:, None, None, None]
    g = jnp.arange(k.shape[2])[None, None, :, None]
    return k[b, idx, g]


def paged_rows(pool, page_table):
    rows = pool[page_table]
    return rows.reshape((rows.shape[0], rows.shape[1] * rows.shape[2]) + rows.shape[3:])


def gather_paged(pool, page_table, new, idx):
    past = page_table.shape[1] * PAGE_SIZE
    b = jnp.arange(idx.shape[0])[:, None, None, None]
    g = jnp.arange(idx.shape[2])[None, None, :, None]
    ip = jnp.minimum(idx, past - 1)
    phys = page_table[b, ip // PAGE_SIZE]
    from_pool = pool[phys, ip % PAGE_SIZE, g]
    from_new = new[b, jnp.clip(idx - past, 0, new.shape[1] - 1), g]
    return jnp.where((idx < past)[..., None], from_pool, from_new)


def mix_queries(pq, pos_q, kc, vc, m_c2s, slc_gather, kw, vw, pos_w, ik, b_gather, k_sel):
    B, T = pq['a_q'].shape[:2]
    qa = pq['a_q'].reshape(B, T, A_GROUPS, A_HEADS // A_GROUPS, HEAD_DIM)
    cmp_end = jnp.arange(kc.shape[1]) * CMP_STRIDE + (CMP_LEN - 1)
    o_cmp, p_cmp = masked_attn(qa, kc, vc, cmp_end[None, :] <= pos_q[:, None])
    n_slc = m_c2s.shape[1]
    blk = jnp.arange(n_slc)[None, :]
    cur = (pos_q // SLC_BLOCK)[:, None]
    forced = (blk == 0) | (blk == cur) | (blk == cur - 1)
    admissible = blk * SLC_BLOCK <= pos_q[:, None]
    imp = jnp.einsum('btgrm,mn->btgn', p_cmp, m_c2s) + jnp.where(forced, FORCE_BONUS, 0.0)[None, :, None, :]
    imp = jnp.where(admissible[None, :, None, :], imp, NEG_INF)
    _, sel = lax.top_k(imp, min(SLC_TOP, n_slc))
    tok = (sel[..., None] * SLC_BLOCK + jnp.arange(SLC_BLOCK)).reshape(B, T, A_GROUPS, -1)
    k_rows, v_rows = slc_gather(tok)
    o_slc = gathered_attn(qa, k_rows, v_rows, tok <= pos_q[None, :, None, None])
    rel = pos_q[:, None] - pos_w[None, :]
    o_win, _ = masked_attn(qa, kw, vw, (rel >= 0) & (rel <= WINDOW) & (pos_w[None, :] >= 0))
    g = pq['a_gate'][..., None]
    hd = (B, T, A_HEADS, HEAD_DIM)
    o_a = g[:, :, 0] * o_cmp.reshape(hd) + g[:, :, 1] * o_slc.reshape(hd) + g[:, :, 2] * o_win.reshape(hd)
    dots = jnp.einsum('bthd,bsd->bths', pq['i_q'], ik)
    score = jnp.einsum('bths,bth->bts', jax.nn.relu(dots), pq['i_w']).astype(jnp.float32)
    visible = jnp.arange(ik.shape[1])[None, :] <= pos_q[:, None]
    score = jnp.where(visible[None], score, NEG_INF)
    _, idx = lax.top_k(score, k_sel)
    idx = jnp.broadcast_to(idx[:, :, None, :], (B, T, B_KV_HEADS, k_sel))
    kb, vb = b_gather(idx)
    qb = pq['b_q'].reshape(B, T, B_KV_HEADS, B_HEADS // B_KV_HEADS, HEAD_DIM)
    o_b = gathered_attn(qb, kb, vb, idx <= pos_q[None, :, None, None]).reshape(B, T, B_HEADS, HEAD_DIM)
    return o_a, o_b


def prompt_mixers(pr, kc, vc, m_c2s, k_sel):
    B, S = pr['a_q'].shape[:2]
    pad = ((0, 0), (WINDOW, 0), (0, 0), (0, 0))
    kw_pad = jnp.pad(pr['a_k_win'], pad)
    vw_pad = jnp.pad(pr['a_v_win'], pad)

    def slc_gather(tok):
        return gather_dense(pr['a_k_slc'], tok), gather_dense(pr['a_v_slc'], tok)

    def b_gather(idx):
        return gather_dense(pr['b_k'], idx), gather_dense(pr['b_v'], idx)

    def block(qb):
        t0 = qb * Q_BLOCK
        pos_q = t0 + jnp.arange(Q_BLOCK)
        pq = {k: lax.dynamic_slice_in_dim(pr[k], t0, Q_BLOCK, axis=1) for k in QUERY_KEYS}
        kw = lax.dynamic_slice_in_dim(kw_pad, t0, Q_BLOCK + WINDOW, axis=1)
        vw = lax.dynamic_slice_in_dim(vw_pad, t0, Q_BLOCK + WINDOW, axis=1)
        pos_w = t0 - WINDOW + jnp.arange(Q_BLOCK + WINDOW)
        return mix_queries(pq, pos_q, kc, vc, m_c2s, slc_gather, kw, vw, pos_w, pr['i_k'], b_gather, k_sel)

    o_a, o_b = lax.map(block, jnp.arange(S // Q_BLOCK))
    o_a = jnp.moveaxis(o_a, 0, 1).reshape(B, S, A_HEADS, HEAD_DIM)
    o_b = jnp.moveaxis(o_b, 0, 1).reshape(B, S, B_HEADS, HEAD_DIM)
    return o_a, o_b


def peer(xn, w_peer_q, peer_subkeys, peer_u, peer_v):
    B, T, D = xn.shape
    n = B * T
    xf = jnp.pad(xn.reshape(n, D), ((0, -n % PEER_CHUNK), (0, 0)))
    half = PEER_KEY_DIM // 2

    def chunk(xc):
        C = xc.shape[0]
        q = jnp.einsum('cd,dhk->chk', xc, w_peer_q)
        s1 = jnp.einsum('chk,hnk->chn', q[..., :half], peer_subkeys[0]).astype(jnp.float32)
        s2 = jnp.einsum('chk,hnk->chn', q[..., half:], peer_subkeys[1]).astype(jnp.float32)
        v1, i1 = lax.top_k(s1, PEER_TOPK)
        v2, i2 = lax.top_k(s2, PEER_TOPK)
        cand = (v1[..., :, None] + v2[..., None, :]).reshape(C, PEER_HEADS, PEER_TOPK * PEER_TOPK)
        cid = (i1[..., :, None] * PEER_NKEYS + i2[..., None, :]).reshape(C, PEER_HEADS, PEER_TOPK * PEER_TOPK)
        top_s, top_pos = lax.top_k(cand, PEER_TOPK)
        eid = jnp.take_along_axis(cid, top_pos, axis=-1)
        gate = jax.nn.softmax(top_s, axis=-1).astype(xc.dtype)
        act = jax.nn.gelu(jnp.einsum('chkd,cd->chk', peer_u[eid], xc), approximate=False)
        return jnp.einsum('chk,chkd->cd', gate * act, peer_v[eid])

    y = lax.map(chunk, xf.reshape(-1, PEER_CHUNK, D)).reshape(-1, D)[:n]
    return y.reshape(B, T, D)


def finish(x, merge, o_a, o_b, w_out_a, w_out_b, w_o, norm2_g, w_peer_q, peer_subkeys, peer_u, peer_v):
    B, T, _ = x.shape
    gates = jax.nn.sigmoid(merge.astype(jnp.float32)).astype(x.dtype)
    ya = jnp.einsum('bte,ed->btd', o_a.reshape(B, T, -1), w_out_a)
    yb = jnp.einsum('bte,ed->btd', o_b.reshape(B, T, -1), w_out_b)
    h = x + jnp.einsum('bte,ed->btd', gates[..., :D_MODEL] * ya + gates[..., D_MODEL:] * yb, w_o)
    return h + peer(rmsnorm(h, norm2_g), w_peer_q, peer_subkeys, peer_u, peer_v)


def setup_inputs(seed: int = 0) -> dict:
    key = jax.random.key(seed)
    ks = iter(jax.random.split(key, 48))

    def nrm(shape, scale=1.0):
        return jax.random.normal(next(ks), shape, jnp.float32) * scale

    def gain(n):
        return 1.0 + 0.01 * jax.random.normal(next(ks), (n,), jnp.float32)

    n_pages = PAST_LEN // PAGE_SIZE
    n_pool = (DEC_BATCH * n_pages * 5) // 4
    win_keep = min(WINDOW, PAST_LEN)
    a_pool = (n_pool, PAGE_SIZE, A_GROUPS, HEAD_DIM)
    b_pool = (n_pool, PAGE_SIZE, B_KV_HEADS, HEAD_DIM)
    win = (DEC_BATCH, win_keep, A_GROUPS, HEAD_DIM)
    page_table = jax.random.permutation(next(ks), n_pool)[:DEC_BATCH * n_pages]
    page_table = page_table.reshape(DEC_BATCH, n_pages).astype(jnp.int32)
    return {
        'x_prompt': nrm((BATCH, SEQ, D_MODEL)),
        'x_sample': nrm((DEC_BATCH, DEC_SEQ, D_MODEL)),
        'cache_a_cmp_k': nrm(a_pool), 'cache_a_cmp_v': nrm(a_pool),
        'cache_a_slc_k': nrm(a_pool), 'cache_a_slc_v': nrm(a_pool),
        'state_a_win_k': nrm(win), 'state_a_win_v': nrm(win),
        'cache_b_k': nrm(b_pool), 'cache_b_v': nrm(b_pool),
        'cache_b_idx_k': nrm((n_pool, PAGE_SIZE, IDX_DIM)),
        'page_table': page_table,
        'norm1_g': gain(D_MODEL),
        'w_in': nrm((D_MODEL, D_IN), D_MODEL ** -0.5),
        'g_qa': gain(HEAD_DIM), 'g_ka_cmp': gain(HEAD_DIM), 'g_ka_slc': gain(HEAD_DIM),
        'g_ka_win': gain(HEAD_DIM), 'g_qb': gain(HEAD_DIM), 'g_kb': gain(HEAD_DIM),
        'g_kidx': gain(IDX_DIM),
        'w_cmp_k': nrm((CMP_LEN, HEAD_DIM, HEAD_DIM), (CMP_LEN * HEAD_DIM) ** -0.5),
        'pe_cmp_k': nrm((CMP_LEN, HEAD_DIM), 0.1),
        'w_cmp_v': nrm((CMP_LEN, HEAD_DIM, HEAD_DIM), (CMP_LEN * HEAD_DIM) ** -0.5),
        'pe_cmp_v': nrm((CMP_LEN, HEAD_DIM), 0.1),
        'g_kc': gain(HEAD_DIM),
        'w_out_a': nrm((A_HEADS * HEAD_DIM, D_MODEL), (A_HEADS * HEAD_DIM) ** -0.5),
        'w_out_b': nrm((B_HEADS * HEAD_DIM, D_MODEL), (B_HEADS * HEAD_DIM) ** -0.5),
        'w_o': nrm((D_MODEL, D_MODEL), D_MODEL ** -0.5),
        'norm2_g': gain(D_MODEL),
        'w_peer_q': nrm((D_MODEL, PEER_HEADS, PEER_KEY_DIM), D_MODEL ** -0.5),
        'peer_subkeys': nrm((2, PEER_HEADS, PEER_NKEYS, PEER_KEY_DIM // 2), (PEER_KEY_DIM // 2) ** -0.5),
        'peer_u': nrm((PEER_EXPERTS, D_MODEL), D_MODEL ** -0.5),
        'peer_v': nrm((PEER_EXPERTS, D_MODEL), PEER_HEADS ** -0.5),
    }


def reference(x_prompt, x_sample, cache_a_cmp_k, cache_a_cmp_v, cache_a_slc_k, cache_a_slc_v,
              state_a_win_k, state_a_win_v, cache_b_k, cache_b_v, cache_b_idx_k, page_table,
              norm1_g, w_in, g_qa, g_ka_cmp, g_ka_slc, g_ka_win, g_qb, g_kb, g_kidx,
              w_cmp_k, pe_cmp_k, w_cmp_v, pe_cmp_v, g_kc, w_out_a, w_out_b, w_o, norm2_g,
              w_peer_q, peer_subkeys, peer_u, peer_v):
    proj_w = (w_in, g_qa, g_ka_cmp, g_ka_slc, g_ka_win, g_qb, g_kb, g_kidx)
    out_w = (w_out_a, w_out_b, w_o, norm2_g, w_peer_q, peer_subkeys, peer_u, peer_v)
    y_prompt, y_sample = x_prompt, x_sample
    for _ in range(DEPTH):
        seq = y_prompt.shape[1]
        pos_p = jnp.arange(seq)
        pr = project(rmsnorm(y_prompt, norm1_g), pos_p, *proj_w)
        kc_p = rmsnorm(compress(pr['a_k_cmp'], w_cmp_k, pe_cmp_k), g_kc)
        vc_p = compress(pr['a_v_cmp'], w_cmp_v, pe_cmp_v)
        m_p = cmp_to_slc(kc_p.shape[1], -(-seq // SLC_BLOCK))
        o_a_p, o_b_p = prompt_mixers(pr, kc_p, vc_p, m_p, min(DSA_TOPK_MAX, seq // 4))
        win_p = min(WINDOW, seq)
        p_a_cmp_k, p_a_cmp_v = pr['a_k_cmp'], pr['a_v_cmp']
        p_a_slc_k, p_a_slc_v = pr['a_k_slc'], pr['a_v_slc']
        p_a_win_k, p_a_win_v = pr['a_k_win'][:, -win_p:], pr['a_v_win'][:, -win_p:]
        p_b_k, p_b_v, p_b_idx_k = pr['b_k'], pr['b_v'], pr['i_k']
        y_prompt = finish(y_prompt, pr['merge'], o_a_p, o_b_p, *out_w)

        n_new = y_sample.shape[1]
        past = page_table.shape[1] * PAGE_SIZE
        total = past + n_new
        pos_s = past + jnp.arange(n_new)
        ps = project(rmsnorm(y_sample, norm1_g), pos_s, *proj_w)
        k_cmp_all = jnp.concatenate([paged_rows(cache_a_cmp_k, page_table), ps['a_k_cmp']], axis=1)
        v_cmp_all = jnp.concatenate([paged_rows(cache_a_cmp_v, page_table), ps['a_v_cmp']], axis=1)
        kc_s = rmsnorm(compress(k_cmp_all, w_cmp_k, pe_cmp_k), g_kc)
        vc_s = compress(v_cmp_all, w_cmp_v, pe_cmp_v)
        m_s = cmp_to_slc(kc_s.shape[1], -(-total // SLC_BLOCK))
        kw_s = jnp.concatenate([state_a_win_k, ps['a_k_win']], axis=1)
        vw_s = jnp.concatenate([state_a_win_v, ps['a_v_win']], axis=1)
        pos_w = past - state_a_win_k.shape[1] + jnp.arange(kw_s.shape[1])
        ik_s = jnp.concatenate([paged_rows(cache_b_idx_k, page_table), ps['i_k']], axis=1)

        def slc_gather_s(tok, ps=ps):
            return (gather_paged(cache_a_slc_k, page_table, ps['a_k_slc'], tok),
                    gather_paged(cache_a_slc_v, page_table, ps['a_v_slc'], tok))

        def b_gather_s(idx, ps=ps):
            return (gather_paged(cache_b_k, page_table, ps['b_k'], idx),
                    gather_paged(cache_b_v, page_table, ps['b_v'], idx))

        o_a_s, o_b_s = mix_queries({k: ps[k] for k in QUERY_KEYS}, pos_s, kc_s, vc_s, m_s, slc_gather_s,
                                   kw_s, vw_s, pos_w, ik_s, b_gather_s, min(DSA_TOPK_MAX, total // 4))
        win_s = min(WINDOW, kw_s.shape[1])
        s_a_cmp_k, s_a_cmp_v = ps['a_k_cmp'], ps['a_v_cmp']
        s_a_slc_k, s_a_slc_v = ps['a_k_slc'], ps['a_v_slc']
        s_a_win_k, s_a_win_v = kw_s[:, -win_s:], vw_s[:, -win_s:]
        s_b_k, s_b_v, s_b_idx_k = ps['b_k'], ps['b_v'], ps['i_k']
        y_sample = finish(y_sample, ps['merge'], o_a_s, o_b_s, *out_w)
    return (y_prompt, y_sample,
            p_a_cmp_k, p_a_cmp_v, p_a_slc_k, p_a_slc_v, p_a_win_k, p_a_win_v, p_b_k, p_b_v, p_b_idx_k,
            s_a_cmp_k, s_a_cmp_v, s_a_slc_k, s_a_slc_v, s_a_win_k, s_a_win_v, s_b_k, s_b_v, s_b_idx_k)
```

```python
import functools

import numpy as np
import jax
import jax.numpy as jnp
from jax import lax
from jax.experimental import pallas as pl
from jax.experimental.pallas import tpu as pltpu

F32 = jnp.float32
BF16 = jnp.bfloat16
I32 = jnp.int32

LANES = 128
HEAD_DIM = 64
ROPE_DIM = HEAD_DIM // 4
ROPE_THETA = 500000.0
QK_SCALE = HEAD_DIM ** -0.5
EPS = 1e-6
NEG_INF = -1e30
A_HEADS = 8
A_GROUPS = 2
CMP_STRIDE = 16
CMP_LEN = 32
SLC_BLOCK = 64
SLC_TOP = 16
FORCE_BONUS = 1e4
WINDOW = 512
B_HEADS = 8
B_KV_HEADS = 2
IDX_HEADS = 4
IDX_DIM = 64
IDX_SCALE = (IDX_HEADS * IDX_DIM) ** -0.5
DSA_TOPK_MAX = 256
PEER_HEADS = 8
PEER_NKEYS = 128
PEER_KEY_DIM = 256
PEER_TOPK = 16
PAGE_SIZE = 128
D_MODEL = 1024

KEY_TILE = 512
VMEM_LIMIT = 56 * 1024 * 1024

OFF_AQ, OFF_KV, OFF_BQ, OFF_BKV, OFF_IQ, OFF_MISC, OFF_MERGE = 0, 512, 1280, 1792, 2048, 2304, 2432
D_PROJ = OFF_MERGE + 2 * D_MODEL
MISC_IW = 64
MISC_GATE = 68

_NT = (((1,), (1,)), ((), ()))


def _dot(a, b):
    return jnp.dot(a, b, preferred_element_type=F32)


def _dot_nt(a, b):
    return lax.dot_general(a, b, _NT, preferred_element_type=F32)


def _split_dot(x, w):
    hi = x.astype(BF16)
    lo = (x - hi.astype(F32)).astype(BF16)
    return _dot(hi, w) + _dot(lo, w)


def _lane(shape):
    return lax.broadcasted_iota(I32, shape, len(shape) - 1)


def _row(shape):
    return lax.broadcasted_iota(I32, shape, len(shape) - 2)


def _order_key(x):
    b = pltpu.bitcast(x + 0.0, I32)
    return b ^ ((b >> 31) & jnp.int32(0x7FFFFFFF))


def _cparams(sem):
    return pltpu.CompilerParams(dimension_semantics=sem, vmem_limit_bytes=VMEM_LIMIT)


def _proj_kernel(x_ref, g1_ref, w_ref, gains_ref, bd_ref, rc_ref, rs1_ref, rs2_ref,
                 aq_ref, bq_ref, iq_ref, kv_ref, bkv_ref, misc_ref, ika_ref, gates_ref,
                 akv_att_ref, bkv_att_ref):
    x = x_ref[0]
    xn = x * lax.rsqrt(jnp.mean(x * x, axis=-1, keepdims=True) + EPS) * g1_ref[...]
    xb = xn.astype(BF16)
    bd = bd_ref[...]
    rc, rs1, rs2 = rc_ref[...], rs1_ref[...], rs2_ref[...]
    lane = _lane((x.shape[0], LANES))
    low = lane < HEAD_DIM

    def proj(a, b):
        return _dot(xb, w_ref[:, a:b])

    def norm(y, g):
        return y * lax.rsqrt(_split_dot(y * y, bd) * (1.0 / HEAD_DIM) + EPS) * g

    def rope(y):
        return y * rc + pltpu.roll(y, LANES - ROPE_DIM // 2, 1) * rs1 + pltpu.roll(y, ROPE_DIM // 2, 1) * rs2

    def att_copy(ref, s, y):
        ref[0, s, 0] = jnp.where(low, y, 0.0).astype(BF16)
        ref[0, s, 1] = jnp.where(low, pltpu.roll(y, HEAD_DIM, 1), 0.0).astype(BF16)

    z = proj(OFF_AQ, OFF_AQ + 512)
    for c in range(4):
        aq_ref[0, :, c * LANES:(c + 1) * LANES] = rope(norm(z[:, c * LANES:(c + 1) * LANES], gains_ref[0:1])) * QK_SCALE
    z = proj(OFF_KV, OFF_KV + 768)
    for s in range(6):
        y = z[:, s * LANES:(s + 1) * LANES]
        if s % 2 == 0:
            y = rope(norm(y, gains_ref[1 + s // 2:2 + s // 2]))
        kv_ref[0, s] = y
        if s >= 2:
            att_copy(akv_att_ref, s - 2, y)
    z = proj(OFF_BQ, OFF_BQ + 512)
    for c in range(4):
        bq_ref[0, :, c * LANES:(c + 1) * LANES] = rope(norm(z[:, c * LANES:(c + 1) * LANES], gains_ref[4:5])) * QK_SCALE
    z = proj(OFF_BKV, OFF_BKV + 256)
    yk = rope(norm(z[:, :LANES], gains_ref[5:6]))
    yv = z[:, LANES:]
    bkv_ref[0, 0] = yk
    bkv_ref[0, 1] = yv
    att_copy(bkv_att_ref, 0, yk)
    att_copy(bkv_att_ref, 1, yv)
    z = proj(OFF_IQ, OFF_IQ + 256)
    for c in range(2):
        iq_ref[0, :, c * LANES:(c + 1) * LANES] = rope(z[:, c * LANES:(c + 1) * LANES])
    z = proj(OFF_MISC, OFF_MISC + LANES)
    ik = rope(norm(z, gains_ref[6:7]))
    misc_ref[0] = jnp.where(low, ik, jnp.where(lane < MISC_GATE, z * IDX_SCALE, jax.nn.sigmoid(z)))
    ika_ref[0] = jnp.where(low, ik, 0.0).astype(BF16)
    gates_ref[0] = jax.nn.sigmoid(proj(OFF_MERGE, OFF_MERGE + 2 * D_MODEL))


def _project(x, pos, norm1_g, wp, gains, tm):
    B, T, D = x.shape
    half = ROPE_DIM // 2
    inv = jnp.power(ROPE_THETA, -jnp.arange(half, dtype=F32) / half)
    ang = pos.astype(F32)[:, None] * inv[None, :]
    cos, sin = jnp.cos(ang), jnp.sin(ang)
    one = jnp.ones((T, HEAD_DIM - ROPE_DIM), F32)
    zero = jnp.zeros((T, HEAD_DIM - ROPE_DIM), F32)
    zh = jnp.zeros((T, half), F32)
    rc = jnp.tile(jnp.concatenate([cos, cos, one], axis=1), (1, 2))
    rs1 = jnp.tile(jnp.concatenate([-sin, zh, zero], axis=1), (1, 2))
    rs2 = jnp.tile(jnp.concatenate([zh, sin, zero], axis=1), (1, 2))
    bd = jnp.asarray(np.kron(np.eye(2), np.ones((HEAD_DIM, HEAD_DIM))), BF16)
    nt = T // tm
    tok = lambda c: pl.BlockSpec((1, tm, c), lambda b, t: (b, t, 0))
    const = lambda shape: pl.BlockSpec(shape, lambda b, t: (0,) * len(shape))
    tab = pl.BlockSpec((tm, LANES), lambda b, t: (t, 0))
    stack = lambda n: pl.BlockSpec((1, n, tm, LANES), lambda b, t: (b, 0, t, 0))
    att = lambda n: pl.BlockSpec((1, n, A_GROUPS, tm, LANES), lambda b, t: (b, 0, 0, t, 0))
    out_shape = (
        jax.ShapeDtypeStruct((B, T, 512), F32),
        jax.ShapeDtypeStruct((B, T, 512), F32),
        jax.ShapeDtypeStruct((B, T, 256), F32),
        jax.ShapeDtypeStruct((B, 6, T, LANES), F32),
        jax.ShapeDtypeStruct((B, 2, T, LANES), F32),
        jax.ShapeDtypeStruct((B, T, LANES), F32),
        jax.ShapeDtypeStruct((B, T, LANES), BF16),
        jax.ShapeDtypeStruct((B, T, 2 * D_MODEL), F32),
        jax.ShapeDtypeStruct((B, 4, A_GROUPS, T, LANES), BF16),
        jax.ShapeDtypeStruct((B, 2, B_KV_HEADS, T, LANES), BF16),
    )
    out_specs = (tok(512), tok(512), tok(256), stack(6), stack(2), tok(LANES), tok(LANES),
                 tok(2 * D_MODEL), att(4), att(2))
    return pl.pallas_call(
        _proj_kernel, grid=(B, nt), out_shape=out_shape, out_specs=out_specs,
        in_specs=[tok(D), const((1, D)), const((D, D_PROJ)), const((8, LANES)), const((LANES, LANES)),
                  tab, tab, tab],
        compiler_params=_cparams(("parallel", "parallel")), name="proj",
    )(x, norm1_g.reshape(1, D), wp, gains, bd, rc, rs1, rs2)


def _compress_kernel(k_ref, v_ref, wk_ref, wv_ref, pek_ref, pev_ref, gkc_ref, kc_ref, vc_ref):
    nch = k_ref.shape[1]

    def one(src_ref, w_ref, pe_ref, out_ref, gain):
        r = _dot(src_ref[0].astype(BF16), w_ref[...])
        rpe = _dot(pe_ref[...].astype(BF16), w_ref[...])
        for g in range(A_GROUPS):
            lo = r[:, g * LANES:(g + 1) * LANES]
            hi = r[:, (2 + g) * LANES:(3 + g) * LANES]
            c = rpe[0:1, g * LANES:(g + 1) * LANES] + rpe[1:2, (2 + g) * LANES:(3 + g) * LANES]
            y = lo + pltpu.roll(hi, nch - 1, 0) + c
            if gain is not None:
                ms = jnp.sum(y * y, axis=-1, keepdims=True) * (1.0 / HEAD_DIM)
                y = y * lax.rsqrt(ms + EPS) * gain
            out_ref[0, g] = y.astype(BF16)

    one(k_ref, wk_ref, pek_ref, kc_ref, gkc_ref[...])
    one(v_ref, wv_ref, pev_ref, vc_ref, None)


def _compress_weights(w_c, pe):
    half = CMP_STRIDE
    w2 = jnp.zeros((half, A_GROUPS, HEAD_DIM, 4, LANES), F32)
    for g in range(A_GROUPS):
        w2 = w2.at[:, g, :, g, :HEAD_DIM].set(w_c[:half])
        w2 = w2.at[:, g, :, 2 + g, :HEAD_DIM].set(w_c[half:])
    w2 = w2.reshape(half * A_GROUPS * HEAD_DIM, 4 * LANES).astype(BF16)
    pe_lo = jnp.tile(pe[:half, None, :], (1, A_GROUPS, 1)).reshape(1, -1)
    pe_hi = jnp.tile(pe[half:, None, :], (1, A_GROUPS, 1)).reshape(1, -1)
    pe2 = jnp.concatenate([pe_lo, pe_hi, jnp.zeros((6, pe_lo.shape[1]), F32)], axis=0)
    return w2, pe2


def _compress(k_chunks, v_chunks, wk2, wv2, pek2, pev2, g_kc):
    B, nch, cw = k_chunks.shape
    gkc = jnp.concatenate([g_kc, jnp.zeros((LANES - HEAD_DIM,), F32)]).reshape(1, LANES)
    src = pl.BlockSpec((1, nch, cw), lambda b: (b, 0, 0))
    const = lambda shape: pl.BlockSpec(shape, lambda b: (0,) * len(shape))
    out = pl.BlockSpec((1, A_GROUPS, nch, LANES), lambda b: (b, 0, 0, 0))
    shp = jax.ShapeDtypeStruct((B, A_GROUPS, nch, LANES), BF16)
    return pl.pallas_call(
        _compress_kernel, grid=(B,), out_shape=(shp, shp), out_specs=(out, out),
        in_specs=[src, src, const(wk2.shape), const(wv2.shape), const(pek2.shape), const(pev2.shape),
                  const((1, LANES))],
        compiler_params=_cparams(("parallel",)), name="compress",
    )(k_chunks, v_chunks, wk2, wv2, pek2, pev2, gkc)


def _heads_rows(q, n_chunks):
    rows = []
    for c in range(n_chunks):
        ch = q[:, c * LANES:(c + 1) * LANES]
        rows.append(ch.astype(BF16))
        rows.append(pltpu.roll(ch, HEAD_DIM, 1).astype(BF16))
    return jnp.concatenate(rows, axis=0)


def _pairs(o, t, n_chunks):
    return [o[(2 * c) * t:(2 * c + 1) * t] + pltpu.roll(o[(2 * c + 1) * t:(2 * c + 2) * t], HEAD_DIM, 1)
            for c in range(n_chunks)]


def _kth_largest(count_ge, k, shape):
    def body(i, t):
        cand = t + lax.shift_left(jnp.int32(1), jnp.int32(31) - i)
        return jnp.where(count_ge(cand) >= k, cand, t)
    return lax.fori_loop(0, 32, body, jnp.full(shape, jnp.iinfo(jnp.int32).min, I32))


def _softmax_rows(s):
    m = jnp.max(s, axis=-1, keepdims=True)
    e = jnp.exp(s - m)
    return e / jnp.sum(e, axis=-1, keepdims=True)


def _nsa_kernel(q_ref, misc_ref, kc_ref, vc_ref, ks_ref, vs_ref, kw_ref, vw_ref, m_ref, tri_ref, eg_ref,
                o_ref, selb_ref, *, tq, nt, pos0, n_cmp, n_slc):
    nch = kc_ref.shape[0]
    nsp = m_ref.shape[1]
    ncs = nsp // LANES
    t0 = pos0 + pl.program_id(2) * tq
    pos = t0 + _row((tq, 1))
    pos4 = jnp.concatenate([pos] * 4, axis=0)
    q4 = _heads_rows(q_ref[...], 2)
    ones = jnp.ones((LANES, LANES), BF16)

    s = _dot_nt(q4, kc_ref[...])
    blk_c = _lane((4 * tq, nch))
    s = jnp.where((blk_c * CMP_STRIDE + (CMP_LEN - 1) <= pos4) & (blk_c < n_cmp), s, NEG_INF)
    p = jnp.where(pos4 >= CMP_LEN - 1, _softmax_rows(s), 0.0)
    o_cmp = _dot(p.astype(BF16), vc_ref[...])
    psum = p[0:tq] + p[tq:2 * tq] + p[2 * tq:3 * tq] + p[3 * tq:4 * tq]
    imp = _split_dot(psum, m_ref[...])

    blk = _lane((tq, nsp))
    cur = pos >> 6
    forced = (blk == 0) | (blk == cur) | (blk == cur - 1)
    adm = (blk * SLC_BLOCK <= pos) & (blk < n_slc)
    key = _order_key(jnp.where(adm, imp + jnp.where(forced, FORCE_BONUS, 0.0), NEG_INF))
    keys = [key[:, c * LANES:(c + 1) * LANES] for c in range(ncs)]

    def count(pred):
        return sum(_dot(jnp.where(pred(kc), 1.0, 0.0).astype(BF16), ones) for kc in keys)

    thr = _kth_largest(lambda cand: count(lambda kc: kc >= cand), float(SLC_TOP), (tq, LANES))
    need = float(SLC_TOP) - count(lambda kc: kc > thr)
    thr_w = jnp.concatenate([thr] * ncs, axis=1)
    need_w = jnp.concatenate([need] * ncs, axis=1)
    eq = key == thr_w
    before = _dot(jnp.where(eq, 1.0, 0.0).astype(BF16), tri_ref[...])
    sel = ((key > thr_w) | (eq & (before < need_w))) & adm
    selb = jnp.where(sel, 0.0, NEG_INF)
    for c in range(ncs):
        selb_ref[c] = selb[:, c * LANES:(c + 1) * LANES]

    klast = t0 // KEY_TILE
    blocks_per_tile = KEY_TILE // SLC_BLOCK
    tiles_per_chunk = LANES // blocks_per_tile

    def step(kt, carry, causal):
        m, l, acc = carry
        sb = selb_ref[kt // tiles_per_chunk].astype(BF16)
        qa = jnp.concatenate([q4, jnp.concatenate([sb] * 4, axis=0)], axis=1)
        off = pl.multiple_of(kt * KEY_TILE, KEY_TILE)
        kblk = (kt % tiles_per_chunk) * blocks_per_tile + (_row((KEY_TILE, LANES)) >> 6)
        et = jnp.where(kblk == _lane((KEY_TILE, LANES)), 1.0, 0.0).astype(BF16)
        ka = jnp.concatenate([ks_ref[pl.ds(off, KEY_TILE), :], et], axis=1)
        sc = _dot_nt(qa, ka)
        if causal:
            sc = jnp.where(off + _lane(sc.shape) <= pos4, sc, NEG_INF)
        m_new = jnp.maximum(m, jnp.max(sc, axis=-1, keepdims=True))
        alpha = jnp.exp(m - m_new)
        pe = jnp.exp(sc - m_new)
        l = alpha * l + jnp.sum(pe, axis=-1, keepdims=True)
        acc = alpha * acc + _dot(pe.astype(BF16), vs_ref[pl.ds(off, KEY_TILE), :])
        return m_new, l, acc

    init = (jnp.full((4 * tq, 1), NEG_INF, F32), jnp.zeros((4 * tq, 1), F32), jnp.zeros((4 * tq, LANES), F32))
    carry = lax.fori_loop(0, klast, lambda kt, c: step(kt, c, False), init)
    _, l, acc = step(klast, carry, True)
    o_slc = acc / l

    wl = kw_ref.shape[0] if nt == 1 else WINDOW + tq
    if nt == 1:
        kw, vw = kw_ref[...], vw_ref[...]
    else:
        ws = pl.multiple_of(pl.program_id(2) * tq, tq)
        kw, vw = kw_ref[pl.ds(ws, wl), :], vw_ref[pl.ds(ws, wl), :]
    sw = _dot_nt(q4, kw)
    pos_w = t0 - WINDOW + _lane(sw.shape)
    rel = pos4 - pos_w
    sw = jnp.where((rel >= 0) & (rel <= WINDOW) & (pos_w >= 0), sw, NEG_INF)
    o_win = _dot(_softmax_rows(sw).astype(BF16), vw)

    gates = _split_dot(misc_ref[...], eg_ref[...])
    pc, ps, pw = _pairs(o_cmp, tq, 2), _pairs(o_slc, tq, 2), _pairs(o_win, tq, 2)
    for c in range(2):
        g = lambda br: gates[:, (2 * br + c) * LANES:(2 * br + c + 1) * LANES]
        o_ref[:, c * LANES:(c + 1) * LANES] = g(0) * pc[c] + g(1) * ps[c] + g(2) * pw[c]


def _cmp_to_slc(nch, nsp, n_cmp, n_slc):
    i = np.arange(nch)[:, None]
    j = np.arange(nsp)[None, :]
    start = i * CMP_STRIDE
    ov = np.minimum(start + CMP_LEN, (j + 1) * SLC_BLOCK) - np.maximum(start, j * SLC_BLOCK)
    m = np.clip(ov, 0, None) / CMP_LEN
    m = np.where((i < n_cmp) & (j < n_slc), m, 0.0)
    return jnp.asarray(m, BF16)


def _gate_expand():
    e = np.zeros((A_GROUPS, LANES, 6 * LANES), np.float32)
    for g in range(A_GROUPS):
        for br in range(3):
            for c in range(2):
                for half in range(2):
                    h = 4 * g + 2 * c + half
                    col = (2 * br + c) * LANES + half * HEAD_DIM
                    e[g, MISC_GATE + br * A_HEADS + h, col:col + HEAD_DIM] = 1.0
    return jnp.asarray(e, BF16)


def _nsa(aq, misc, kc, vc, att, kw, vw, *, tq, pos0, n_cmp, n_slc):
    B, T, _ = aq.shape
    nch = kc.shape[2]
    Lk = att.shape[3]
    nsp = -(-max(n_slc, Lk // SLC_BLOCK) // LANES) * LANES
    m = _cmp_to_slc(nch, nsp, n_cmp, n_slc)
    tri = jnp.asarray(np.triu(np.ones((nsp, nsp)), 1), BF16)
    eg = _gate_expand()
    Lw = kw.shape[2]
    nt = T // tq
    in_specs = [
        pl.BlockSpec((None, tq, 256), lambda b, g, t: (b, t, g)),
        pl.BlockSpec((None, tq, LANES), lambda b, g, t: (b, t, 0)),
        pl.BlockSpec((None, None, nch, LANES), lambda b, g, t: (b, g, 0, 0)),
        pl.BlockSpec((None, None, nch, LANES), lambda b, g, t: (b, g, 0, 0)),
        pl.BlockSpec((None, None, None, Lk, LANES), lambda b, g, t: (b, 0, g, 0, 0)),
        pl.BlockSpec((None, None, None, Lk, LANES), lambda b, g, t: (b, 1, g, 0, 0)),
        pl.BlockSpec((None, None, Lw, LANES), lambda b, g, t: (b, g, 0, 0)),
        pl.BlockSpec((None, None, Lw, LANES), lambda b, g, t: (b, g, 0, 0)),
        pl.BlockSpec(m.shape, lambda b, g, t: (0, 0)),
        pl.BlockSpec(tri.shape, lambda b, g, t: (0, 0)),
        pl.BlockSpec((None, LANES, 6 * LANES), lambda b, g, t: (g, 0, 0)),
    ]
    kern = functools.partial(_nsa_kernel, tq=tq, nt=nt, pos0=pos0, n_cmp=n_cmp, n_slc=n_slc)
    return pl.pallas_call(
        kern, grid=(B, A_GROUPS, nt), in_specs=in_specs,
        out_specs=pl.BlockSpec((None, tq, 256), lambda b, g, t: (b, t, g)),
        out_shape=jax.ShapeDtypeStruct((B, T, 512), F32),
        scratch_shapes=[pltpu.VMEM((nsp // LANES, tq, LANES), F32)],
        compiler_params=_cparams(("parallel", "parallel", "arbitrary")), name="nsa",
    )(aq, misc, kc, vc, att, att, kw, vw, m, tri, eg)


def _dsa_kernel(iq_ref, bq_ref, misc_ref, ik_ref, bk_ref, bv_ref, tri_ref, o_ref, key_ref, *, tq, pos0, k_sel):
    t0 = pos0 + pl.program_id(1) * tq
    pos = t0 + _row((tq, 1))
    nkt = t0 // KEY_TILE + 1
    reps = KEY_TILE // LANES
    qi4 = _heads_rows(iq_ref[...], 2)
    misc = misc_ref[...]
    wts = [misc[:, MISC_IW + h:MISC_IW + h + 1] for h in range(4)]
    ones = jnp.ones((KEY_TILE, LANES), BF16)
    lane_t = _lane((tq, KEY_TILE))

    def score_step(kt, _):
        off = pl.multiple_of(kt * KEY_TILE, KEY_TILE)
        d = jnp.maximum(_dot_nt(qi4, ik_ref[pl.ds(off, KEY_TILE), :]), 0.0)
        sc = d[0:tq] * wts[0] + d[tq:2 * tq] * wts[1] + d[2 * tq:3 * tq] * wts[2] + d[3 * tq:4 * tq] * wts[3]
        key_ref[kt] = _order_key(jnp.where(off + lane_t <= pos, sc, NEG_INF))
        return 0

    lax.fori_loop(0, nkt, score_step, 0)

    def count(pred, ref_val):
        wide = jnp.concatenate([ref_val] * reps, axis=1)

        def body(kt, c):
            return c + _dot(jnp.where(pred(key_ref[kt], wide), 1.0, 0.0).astype(BF16), ones)
        return lax.fori_loop(0, nkt, body, jnp.zeros((tq, LANES), F32))

    thr = _kth_largest(lambda cand: count(lambda k, w: k >= w, cand), float(k_sel), (tq, LANES))
    need = float(k_sel) - count(lambda k, w: k > w, thr)
    thr_w = jnp.concatenate([thr] * reps, axis=1)
    need_w = jnp.concatenate([need] * reps, axis=1)

    q8 = [_heads_rows(bq_ref[:, g * 256:(g + 1) * 256], 2) for g in range(B_KV_HEADS)]
    r4 = 4 * tq

    def step(kt, carry):
        m, l, acc, seen = carry
        off = pl.multiple_of(kt * KEY_TILE, KEY_TILE)
        key = key_ref[kt]
        eq = key == thr_w
        eqb = jnp.where(eq, 1.0, 0.0).astype(BF16)
        before = seen + _dot(eqb, tri_ref[...])
        sel = ((key > thr_w) | (eq & (before < need_w))) & (off + lane_t <= pos)
        bias = jnp.where(sel, 0.0, NEG_INF)
        bias4 = jnp.concatenate([bias] * 4, axis=0)
        sc = jnp.concatenate([_dot_nt(q8[g], bk_ref[g, pl.ds(off, KEY_TILE), :]) + bias4
                              for g in range(B_KV_HEADS)], axis=0)
        m_new = jnp.maximum(m, jnp.max(sc, axis=-1, keepdims=True))
        alpha = jnp.exp(m - m_new)
        pe = jnp.exp(sc - m_new)
        l = alpha * l + jnp.sum(pe, axis=-1, keepdims=True)
        pb = pe.astype(BF16)
        pv = jnp.concatenate([_dot(pb[g * r4:(g + 1) * r4], bv_ref[g, pl.ds(off, KEY_TILE), :])
                              for g in range(B_KV_HEADS)], axis=0)
        seen = seen + jnp.sum(jnp.where(eq, 1.0, 0.0), axis=-1, keepdims=True)
        return m_new, l, alpha * acc + pv, seen

    init = (jnp.full((2 * r4, 1), NEG_INF, F32), jnp.zeros((2 * r4, 1), F32),
            jnp.zeros((2 * r4, LANES), F32), jnp.zeros((tq, 1), F32))
    _, l, acc, _ = lax.fori_loop(0, nkt, step, init)
    o = _pairs(acc / l, tq, 4)
    for c in range(4):
        o_ref[:, c * LANES:(c + 1) * LANES] = o[c]


def _dsa(iq, bq, misc, ika, batt, *, tq, pos0, k_sel):
    B, T, _ = bq.shape
    Lk = ika.shape[1]
    nt = T // tq
    tri = jnp.asarray(np.triu(np.ones((KEY_TILE, KEY_TILE)), 1), BF16)
    in_specs = [
        pl.BlockSpec((None, tq, 256), lambda b, t: (b, t, 0)),
        pl.BlockSpec((None, tq, 512), lambda b, t: (b, t, 0)),
        pl.BlockSpec((None, tq, LANES), lambda b, t: (b, t, 0)),
        pl.BlockSpec((None, Lk, LANES), lambda b, t: (b, 0, 0)),
        pl.BlockSpec((None, None, B_KV_HEADS, Lk, LANES), lambda b, t: (b, 0, 0, 0, 0)),
        pl.BlockSpec((None, None, B_KV_HEADS, Lk, LANES), lambda b, t: (b, 1, 0, 0, 0)),
        pl.BlockSpec(tri.shape, lambda b, t: (0, 0)),
    ]
    kern = functools.partial(_dsa_kernel, tq=tq, pos0=pos0, k_sel=k_sel)
    return pl.pallas_call(
        kern, grid=(B, nt), in_specs=in_specs,
        out_specs=pl.BlockSpec((None, tq, 512), lambda b, t: (b, t, 0)),
        out_shape=jax.ShapeDtypeStruct((B, T, 512), F32),
        scratch_shapes=[pltpu.VMEM((Lk // KEY_TILE, tq, KEY_TILE), I32)],
        compiler_params=_cparams(("parallel", "arbitrary")), name="dsa",
    )(iq, bq, misc, ika, batt, batt, tri)


def _gather_kernel(pt_ref, *refs, n, npg):
    pools, outs, sem = refs[:n], refs[n:2 * n], refs[2 * n]
    b = pl.program_id(0)

    def copy(i, page, p):
        return pltpu.make_async_copy(pools[i].at[page], outs[i].at[b, p], sem.at[i])

    def issue(p, _):
        page = pt_ref[b, p]
        for i in range(n):
            copy(i, page, p).start()
        return 0

    def wait(p, _):
        for i in range(n):
            copy(i, 0, p).wait()
        return 0

    lax.fori_loop(0, npg, issue, 0)
    lax.fori_loop(0, npg, wait, 0)


def _gather_pages(pools, page_table):
    n = len(pools)
    DB, npg = page_table.shape
    any_spec = pl.BlockSpec(memory_space=pl.ANY)
    grid_spec = pltpu.PrefetchScalarGridSpec(
        num_scalar_prefetch=1, grid=(DB,), in_specs=[any_spec] * n, out_specs=[any_spec] * n,
        scratch_shapes=[pltpu.SemaphoreType.DMA((n,))])
    out_shape = [jax.ShapeDtypeStruct((DB, npg) + p.shape[1:], p.dtype) for p in pools]
    return pl.pallas_call(
        functools.partial(_gather_kernel, n=n, npg=npg), grid_spec=grid_spec, out_shape=out_shape,
        compiler_params=pltpu.CompilerParams(dimension_semantics=("arbitrary",)), name="gather_pages",
    )(page_table, *pools)


def _topk_rows(s, k, payload=None):
    n = s.shape[0]
    rows = _row(s.shape)
    vals, idxs = [], []
    for _ in range(k):
        m = jnp.max(s, axis=0, keepdims=True)
        idx = jnp.min(jnp.where(s == m, rows, n), axis=0, keepdims=True)
        hit = rows == idx
        vals.append(m)
        idxs.append(idx if payload is None else jnp.max(jnp.where(hit, payload, -1), axis=0, keepdims=True))
        s = jnp.where(hit, -jnp.inf, s)
    return jnp.concatenate(vals, axis=0), jnp.concatenate(idxs, axis=0)


def _route_kernel(x_ref, oa_ref, ob_ref, gates_ref, woa_ref, wob_ref, wo_ref, g2_ref, wq_ref, sub1_ref, sub2_ref,
                  h_ref, hn_ref, eid_ref, gate_ref):
    gates = gates_ref[...]
    ya = _dot(oa_ref[...].astype(BF16), woa_ref[...])
    yb = _dot(ob_ref[...].astype(BF16), wob_ref[...])
    mix = gates[:, :D_MODEL] * ya + gates[:, D_MODEL:] * yb
    h = x_ref[...] + _dot(mix.astype(BF16), wo_ref[...])
    h_ref[...] = h
    hn = h * lax.rsqrt(jnp.mean(h * h, axis=-1, keepdims=True) + EPS) * g2_ref[...]
    hn_ref[...] = hn
    q = _dot(hn.astype(BF16), wq_ref[...])
    half = PEER_KEY_DIM // 2
    for hd in range(PEER_HEADS):
        q1 = q[:, hd * PEER_KEY_DIM:hd * PEER_KEY_DIM + half].astype(BF16)
        q2 = q[:, hd * PEER_KEY_DIM + half:(hd + 1) * PEER_KEY_DIM].astype(BF16)
        v1, i1 = _topk_rows(_dot_nt(sub1_ref[hd], q1), PEER_TOPK)
        v2, i2 = _topk_rows(_dot_nt(sub2_ref[hd], q2), PEER_TOPK)
        cand = jnp.concatenate([v1[r:r + 1] + v2 for r in range(PEER_TOPK)], axis=0)
        cid = jnp.concatenate([i1[r:r + 1] * PEER_NKEYS + i2 for r in range(PEER_TOPK)], axis=0)
        vt, et = _topk_rows(cand, PEER_TOPK, payload=cid)
        e = jnp.exp(vt - vt[0:1])
        eid_ref[0, hd * PEER_TOPK:(hd + 1) * PEER_TOPK, :] = et
        gate_ref[0, hd * PEER_TOPK:(hd + 1) * PEER_TOPK, :] = e / jnp.sum(e, axis=0, keepdims=True)


def _route(x, oa, ob, gates, woa, wob, wo, g2, wq, sub1, sub2, tm):
    N, D = x.shape
    nt = N // tm
    tok = lambda c: pl.BlockSpec((tm, c), lambda i: (i, 0))
    const = lambda a: pl.BlockSpec(a.shape, lambda i: (0,) * a.ndim)
    slot = pl.BlockSpec((1, PEER_HEADS * PEER_TOPK, tm), lambda i: (i, 0, 0))
    consts = (woa, wob, wo, g2, wq, sub1, sub2)
    return pl.pallas_call(
        _route_kernel, grid=(nt,),
        in_specs=[tok(D), tok(512), tok(512), tok(2 * D)] + [const(a) for a in consts],
        out_specs=(tok(D), tok(D), slot, slot),
        out_shape=(jax.ShapeDtypeStruct((N, D), F32), jax.ShapeDtypeStruct((N, D), F32),
                   jax.ShapeDtypeStruct((nt, PEER_HEADS * PEER_TOPK, tm), I32),
                   jax.ShapeDtypeStruct((nt, PEER_HEADS * PEER_TOPK, tm), F32)),
        compiler_params=_cparams(("parallel",)), name="route",
    )(x, oa, ob, gates, *consts)


def _pack_table(t):
    E, D = t.shape
    b = lax.bitcast_convert_type(t.astype(BF16), jnp.uint16).astype(jnp.uint32)
    w = b[:, :D // 2] | (b[:, D // 2:] << 16)
    return lax.bitcast_convert_type(w, I32).reshape(E, D // 256, LANES)


def _unpack_row(w):
    return pltpu.bitcast(w << 16, F32), pltpu.bitcast(w & jnp.int32(-65536), F32)


def _load_table(tab_hbm, tab_vmem, sem):
    @pl.when(pl.program_id(0) == 0)
    def _():
        cp = pltpu.make_async_copy(tab_hbm, tab_vmem, sem)
        cp.start()
        cp.wait()


def _peer_act_kernel(eid_ref, x_ref, gate_ref, tab_hbm, w_ref, tab_vmem, sem, *, tt):
    _load_table(tab_hbm, tab_vmem, sem)
    n_slot = gate_ref.shape[1]
    half = x_ref.shape[1] // 2
    lane = _lane((1, tt))

    def token(t, act):
        x = x_ref[t]
        xlo, xhi = x[0:half], x[half:]
        rows = []
        for j in range(n_slot):
            lo, hi = _unpack_row(tab_vmem[eid_ref[j * tt + t]])
            rows.append(jnp.sum(lo * xlo + hi * xhi, axis=0, keepdims=True))
        a = jnp.sum(jnp.concatenate(rows, axis=0), axis=1, keepdims=True)
        return act + a * jnp.where(lane == t, 1.0, 0.0)

    act = lax.fori_loop(0, tt, token, jnp.zeros((n_slot, tt), F32))
    w_ref[0] = gate_ref[0] * (act * (lax.erf(act * (2.0 ** -0.5)) + 1.0) * 0.5)


def _peer_out_kernel(eid_ref, w_ref, h_ref, tab_hbm, y_ref, tab_vmem, sem, *, tt, n_slot):
    _load_table(tab_hbm, tab_vmem, sem)
    half = h_ref.shape[1] // 2

    def token(t, _):
        acc = [jnp.zeros((half, LANES), F32) for _ in range(4)]
        for j in range(n_slot):
            lo, hi = _unpack_row(tab_vmem[eid_ref[j * tt + t]])
            wj = w_ref[j * tt + t]
            acc[2 * (j % 2)] = acc[2 * (j % 2)] + wj * lo
            acc[2 * (j % 2) + 1] = acc[2 * (j % 2) + 1] + wj * hi
        y_ref[t] = h_ref[t] + jnp.concatenate([acc[0] + acc[2], acc[1] + acc[3]], axis=0)
        return 0

    lax.fori_loop(0, tt, token, 0)


def _peer(eid, gate, hn, h, u_tab, v_tab, tt):
    nt, n_slot, _ = eid.shape
    N, D = h.shape
    sub = D // LANES
    eid_flat = eid.reshape(-1)
    smem = lambda: pl.BlockSpec((n_slot * tt,), lambda i: (i,), memory_space=pltpu.SMEM)
    rows = pl.BlockSpec((tt, sub, LANES), lambda i: (i, 0, 0))
    slot = pl.BlockSpec((1, n_slot, tt), lambda i: (i, 0, 0))
    any_spec = pl.BlockSpec(memory_space=pl.ANY)
    scratch = [pltpu.VMEM(u_tab.shape, I32), pltpu.SemaphoreType.DMA(())]
    w = pl.pallas_call(
        functools.partial(_peer_act_kernel, tt=tt), grid=(nt,),
        in_specs=[smem(), rows, slot, any_spec], out_specs=slot,
        out_shape=jax.ShapeDtypeStruct((nt, n_slot, tt), F32), scratch_shapes=scratch,
        compiler_params=_cparams(("arbitrary",)), name="peer_act",
    )(eid_flat, hn.reshape(N, sub, LANES), gate, u_tab)
    y = pl.pallas_call(
        functools.partial(_peer_out_kernel, tt=tt, n_slot=n_slot), grid=(nt,),
        in_specs=[smem(), smem(), rows, any_spec], out_specs=rows,
        out_shape=jax.ShapeDtypeStruct((N, sub, LANES), F32), scratch_shapes=scratch,
        compiler_params=_cparams(("arbitrary",)), name="peer_out",
    )(eid_flat, w.reshape(-1), h.reshape(N, sub, LANES), v_tab)
    return y.reshape(N, D)


def _relayout_w_in(w_in):
    names = (('a_q', 512), ('a_k_cmp', 128), ('a_v_cmp', 128), ('a_k_slc', 128), ('a_v_slc', 128),
             ('a_k_win', 128), ('a_v_win', 128), ('a_gate', 24), ('b_q', 512), ('b_k', 128), ('b_v', 128),
             ('i_q', 256), ('i_w', 4), ('i_k', 64), ('merge', 2 * D_MODEL))
    sec, o = {}, 0
    for name, n in names:
        sec[name] = w_in[:, o:o + n]
        o += n
    pad = jnp.zeros((w_in.shape[0], LANES - 64 - 4 - 24), w_in.dtype)
    order = ['a_q', 'a_k_cmp', 'a_v_cmp', 'a_k_slc', 'a_v_slc', 'a_k_win', 'a_v_win', 'b_q', 'b_k', 'b_v',
             'i_q', 'i_k', 'i_w', 'a_gate']
    return jnp.concatenate([sec[n] for n in order] + [pad, sec['merge']], axis=1).astype(BF16)


def _group_lanes(x):
    z = jnp.zeros_like(x[..., :HEAD_DIM])
    parts = [jnp.concatenate([x[..., g * HEAD_DIM:(g + 1) * HEAD_DIM], z], axis=-1) for g in range(2)]
    return jnp.stack(parts, axis=-3).astype(BF16)


def _pad_rows(x, total, axis):
    pad = [(0, 0)] * x.ndim
    pad[axis] = (0, total - x.shape[axis])
    return jnp.pad(x, pad)


def kernel(x_prompt, x_sample, cache_a_cmp_k, cache_a_cmp_v, cache_a_slc_k, cache_a_slc_v, state_a_win_k, state_a_win_v, cache_b_k, cache_b_v, cache_b_idx_k, page_table, norm1_g, w_in, g_qa, g_ka_cmp, g_ka_slc, g_ka_win, g_qb, g_kb, g_kidx, w_cmp_k, pe_cmp_k, w_cmp_v, pe_cmp_v, g_kc, w_out_a, w_out_b, w_o, norm2_g, w_peer_q, peer_subkeys, peer_u, peer_v):
    B, S, D = x_prompt.shape
    DB, NS, _ = x_sample.shape
    npg = page_table.shape[1]
    past = npg * PAGE_SIZE
    assert D == D_MODEL and S % KEY_TILE == 0 and past % KEY_TILE == 0 and NS < CMP_STRIDE
    assert state_a_win_k.shape[1] == WINDOW

    wp = _relayout_w_in(w_in)
    ones = jnp.ones((HEAD_DIM,), F32)
    gains = jnp.stack([jnp.tile(g, 2) for g in (g_qa, g_ka_cmp, g_ka_slc, g_ka_win, g_qb, g_kb)]
                      + [jnp.concatenate([g_kidx, ones]), jnp.tile(ones, 2)])
    wk2, pek2 = _compress_weights(w_cmp_k, pe_cmp_k)
    wv2, pev2 = _compress_weights(w_cmp_v, pe_cmp_v)
    woa, wob, wo = w_out_a.astype(BF16), w_out_b.astype(BF16), w_o.astype(BF16)
    wq = w_peer_q.reshape(D, PEER_HEADS * PEER_KEY_DIM).astype(BF16)
    sub1, sub2 = peer_subkeys[0].astype(BF16), peer_subkeys[1].astype(BF16)
    g2 = norm2_g.reshape(1, D)
    u_tab, v_tab = _pack_table(peer_u), _pack_table(peer_v)

    def finish(x, oa, ob, gates):
        n = x.shape[0] * x.shape[1]
        tt = 128 if n % 128 == 0 else n
        h, hn, eid, gate = _route(x.reshape(n, D), oa.reshape(n, 512), ob.reshape(n, 512),
                                  gates.reshape(n, 2 * D), woa, wob, wo, g2, wq, sub1, sub2, tt)
        return _peer(eid, gate, hn, h, u_tab, v_tab, tt).reshape(x.shape)

    def heads(a, n):
        return a.reshape(a.shape[0], a.shape[1], n, HEAD_DIM)

    tq = 128
    aq, bq, iq, kv, bkv, misc, ika, gates, a_att, b_att = _project(
        x_prompt, jnp.arange(S), norm1_g, wp, gains, 256)
    nch = S // CMP_STRIDE
    chunks = kv.reshape(B, 6, nch, CMP_STRIDE * LANES)
    kc, vc = _compress(chunks[:, 0], chunks[:, 1], wk2, wv2, pek2, pev2, g_kc)
    front = ((0, 0), (0, 0), (WINDOW, 0), (0, 0))
    o_a = _nsa(aq, misc, kc, vc, a_att, jnp.pad(a_att[:, 2], front), jnp.pad(a_att[:, 3], front),
               tq=tq, pos0=0, n_cmp=nch - 1, n_slc=-(-S // SLC_BLOCK))
    o_b = _dsa(iq, bq, misc, ika, b_att, tq=tq, pos0=0, k_sel=min(DSA_TOPK_MAX, S // 4))
    y_prompt = finish(x_prompt, o_a, o_b, gates)
    win_p = min(WINDOW, S)
    p_out = [heads(kv[:, i], A_GROUPS) for i in range(4)]
    p_out += [heads(kv[:, i, S - win_p:], A_GROUPS) for i in (4, 5)]
    p_out += [heads(bkv[:, 0], B_KV_HEADS), heads(bkv[:, 1], B_KV_HEADS), misc[:, :, :IDX_DIM]]

    total = past + NS
    lk = -(-total // KEY_TILE) * KEY_TILE
    aq, bq, iq, kv, bkv, misc, ika, gates, a_att, b_att = _project(
        x_sample, past + jnp.arange(NS), norm1_g, wp, gains, NS)
    flat = lambda c: c.reshape(c.shape[0], PAGE_SIZE, -1)
    rows = _gather_pages([flat(c) for c in (cache_a_cmp_k, cache_a_cmp_v, cache_a_slc_k, cache_a_slc_v,
                                            cache_b_k, cache_b_v, cache_b_idx_k)], page_table)
    rows = [r.reshape(DB, past, r.shape[-1]) for r in rows]
    nch_s = (total // CMP_STRIDE)
    assert nch_s * CMP_STRIDE <= past
    kc, vc = _compress(rows[0].reshape(DB, nch_s, CMP_STRIDE * LANES), rows[1].reshape(DB, nch_s, CMP_STRIDE * LANES),
                       wk2, wv2, pek2, pev2, g_kc)

    def with_new(past_rows, new_att):
        return _pad_rows(jnp.concatenate([_group_lanes(past_rows), new_att], axis=2), lk, 2)

    a_att_s = jnp.stack([with_new(rows[2], a_att[:, 0]), with_new(rows[3], a_att[:, 1])], axis=1)
    b_att_s = jnp.stack([with_new(rows[4], b_att[:, 0]), with_new(rows[5], b_att[:, 1])], axis=1)
    ik_past = jnp.concatenate([rows[6], jnp.zeros_like(rows[6])], axis=-1).astype(BF16)
    ika_s = _pad_rows(jnp.concatenate([ik_past, ika], axis=1), lk, 1)
    wl = -(-(WINDOW + NS) // LANES) * LANES
    kw_s = _pad_rows(jnp.concatenate([_group_lanes(state_a_win_k.reshape(DB, WINDOW, LANES)), a_att[:, 2]], axis=2), wl, 2)
    vw_s = _pad_rows(jnp.concatenate([_group_lanes(state_a_win_v.reshape(DB, WINDOW, LANES)), a_att[:, 3]], axis=2), wl, 2)
    o_a = _nsa(aq, misc, kc, vc, a_att_s, kw_s, vw_s, tq=NS, pos0=past, n_cmp=nch_s - 1,
               n_slc=-(-total // SLC_BLOCK))
    o_b = _dsa(iq, bq, misc, ika_s, b_att_s, tq=NS, pos0=past, k_sel=min(DSA_TOPK_MAX, total // 4))
    y_sample = finish(x_sample, o_a, o_b, gates)
    s_out = [heads(kv[:, i], A_GROUPS) for i in range(4)]
    s_out += [jnp.concatenate([state_a_win_k[:, NS:], heads(kv[:, 4], A_GROUPS)], axis=1),
              jnp.concatenate([state_a_win_v[:, NS:], heads(kv[:, 5], A_GROUPS)], axis=1)]
    s_out += [heads(bkv[:, 0], B_KV_HEADS), heads(bkv[:, 1], B_KV_HEADS), misc[:, :, :IDX_DIM]]
    return (y_prompt, y_sample, *p_out, *s_out)
```

```python
import functools

import numpy as np
import jax
import jax.numpy as jnp
from jax import lax
from jax.experimental import pallas as pl
from jax.experimental.pallas import tpu as pltpu

F32 = jnp.float32
BF16 = jnp.bfloat16
I32 = jnp.int32

LANES = 128
HEAD_DIM = 64
ROPE_DIM = HEAD_DIM // 4
ROPE_THETA = 500000.0
QK_SCALE = HEAD_DIM ** -0.5
EPS = 1e-6
NEG_INF = -1e30
A_HEADS = 8
A_GROUPS = 2
CMP_STRIDE = 16
CMP_LEN = 32
SLC_BLOCK = 64
SLC_TOP = 16
FORCE_BONUS = 1e4
WINDOW = 512
B_HEADS = 8
B_KV_HEADS = 2
IDX_HEADS = 4
IDX_DIM = 64
IDX_SCALE = (IDX_HEADS * IDX_DIM) ** -0.5
DSA_TOPK_MAX = 256
PEER_HEADS = 8
PEER_NKEYS = 128
PEER_KEY_DIM = 256
PEER_TOPK = 16
PAGE_SIZE = 128
D_MODEL = 1024

KEY_TILE = 512
VMEM_LIMIT = 56 * 1024 * 1024

OFF_AQ, OFF_KV, OFF_BQ, OFF_BKV, OFF_IQ, OFF_MISC, OFF_MERGE = 0, 512, 1280, 1792, 2048, 2304, 2432
D_PROJ = OFF_MERGE + 2 * D_MODEL
MISC_IW = 64
MISC_GATE = 68

_NT = (((1,), (1,)), ((), ()))


def _dot(a, b):
    return jnp.dot(a, b, preferred_element_type=F32)


def _dot_nt(a, b):
    return lax.dot_general(a, b, _NT, preferred_element_type=F32)


def _split_dot(x, w):
    hi = x.astype(BF16)
    lo = (x - hi.astype(F32)).astype(BF16)
    return _dot(hi, w) + _dot(lo, w)


def _lane(shape):
    return lax.broadcasted_iota(I32, shape, len(shape) - 1)


def _row(shape):
    return lax.broadcasted_iota(I32, shape, len(shape) - 2)


def _order_key(x):
    b = pltpu.bitcast(x + 0.0, I32)
    return b ^ ((b >> 31) & jnp.int32(0x7FFFFFFF))


def _cparams(sem):
    return pltpu.CompilerParams(dimension_semantics=sem, vmem_limit_bytes=VMEM_LIMIT)


def _single(shape, index_map):
    return pl.BlockSpec(shape, index_map, pipeline_mode=pl.Buffered(1))


def _proj_kernel(x_ref, g1_ref, w_ref, gains_ref, bd_ref, rc_ref, rs1_ref, rs2_ref,
                 aq_ref, bq_ref, iq_ref, kv_ref, bkv_ref, misc_ref, ika_ref, gates_ref,
                 akv_att_ref, bkv_att_ref):
    x = x_ref[0]
    xn = x * lax.rsqrt(jnp.mean(x * x, axis=-1, keepdims=True) + EPS) * g1_ref[...]
    xb = xn.astype(BF16)
    bd = bd_ref[...]
    rc, rs1, rs2 = rc_ref[...], rs1_ref[...], rs2_ref[...]
    lane = _lane((x.shape[0], LANES))
    low = lane < HEAD_DIM

    def proj(a, b):
        return _dot(xb, w_ref[:, a:b])

    def norm(y, g):
        return y * lax.rsqrt(_split_dot(y * y, bd) * (1.0 / HEAD_DIM) + EPS) * g

    def rope(y):
        return y * rc + pltpu.roll(y, LANES - ROPE_DIM // 2, 1) * rs1 + pltpu.roll(y, ROPE_DIM // 2, 1) * rs2

    def att_copy(ref, s, y):
        ref[0, s, 0] = jnp.where(low, y, 0.0).astype(BF16)
        ref[0, s, 1] = jnp.where(low, pltpu.roll(y, HEAD_DIM, 1), 0.0).astype(BF16)

    z = proj(OFF_AQ, OFF_AQ + 512)
    for c in range(4):
        aq_ref[0, :, c * LANES:(c + 1) * LANES] = rope(norm(z[:, c * LANES:(c + 1) * LANES], gains_ref[0:1])) * QK_SCALE
    z = proj(OFF_KV, OFF_KV + 768)
    for s in range(6):
        y = z[:, s * LANES:(s + 1) * LANES]
        if s % 2 == 0:
            y = rope(norm(y, gains_ref[1 + s // 2:2 + s // 2]))
        kv_ref[0, s] = y
        if s >= 2:
            att_copy(akv_att_ref, s - 2, y)
    z = proj(OFF_BQ, OFF_BQ + 512)
    for c in range(4):
        bq_ref[0, :, c * LANES:(c + 1) * LANES] = rope(norm(z[:, c * LANES:(c + 1) * LANES], gains_ref[4:5])) * QK_SCALE
    z = proj(OFF_BKV, OFF_BKV + 256)
    yk = rope(norm(z[:, :LANES], gains_ref[5:6]))
    yv = z[:, LANES:]
    bkv_ref[0, 0] = yk
    bkv_ref[0, 1] = yv
    att_copy(bkv_att_ref, 0, yk)
    att_copy(bkv_att_ref, 1, yv)
    z = proj(OFF_IQ, OFF_IQ + 256)
    for c in range(2):
        iq_ref[0, :, c * LANES:(c + 1) * LANES] = rope(z[:, c * LANES:(c + 1) * LANES])
    z = proj(OFF_MISC, OFF_MISC + LANES)
    ik = rope(norm(z, gains_ref[6:7]))
    misc_ref[0] = jnp.where(low, ik, jnp.where(lane < MISC_GATE, z * IDX_SCALE, jax.nn.sigmoid(z)))
    ika_ref[0] = jnp.where(low, ik, 0.0).astype(BF16)
    gates_ref[0] = jax.nn.sigmoid(proj(OFF_MERGE, OFF_MERGE + 2 * D_MODEL))


def _project(x, pos, norm1_g, wp, gains, tm):
    B, T, D = x.shape
    half = ROPE_DIM // 2
    inv = jnp.power(ROPE_THETA, -jnp.arange(half, dtype=F32) / half)
    ang = pos.astype(F32)[:, None] * inv[None, :]
    cos, sin = jnp.cos(ang), jnp.sin(ang)
    one = jnp.ones((T, HEAD_DIM - ROPE_DIM), F32)
    zero = jnp.zeros((T, HEAD_DIM - ROPE_DIM), F32)
    zh = jnp.zeros((T, half), F32)
    rc = jnp.tile(jnp.concatenate([cos, cos, one], axis=1), (1, 2))
    rs1 = jnp.tile(jnp.concatenate([-sin, zh, zero], axis=1), (1, 2))
    rs2 = jnp.tile(jnp.concatenate([zh, sin, zero], axis=1), (1, 2))
    bd = jnp.asarray(np.kron(np.eye(2), np.ones((HEAD_DIM, HEAD_DIM))), BF16)
    nt = T // tm
    tok = lambda c: pl.BlockSpec((1, tm, c), lambda b, t: (b, t, 0))
    const = lambda shape: pl.BlockSpec(shape, lambda b, t: (0,) * len(shape))
    tab = pl.BlockSpec((tm, LANES), lambda b, t: (t, 0))
    stack = lambda n: pl.BlockSpec((1, n, tm, LANES), lambda b, t: (b, 0, t, 0))
    att = lambda n: pl.BlockSpec((1, n, A_GROUPS, tm, LANES), lambda b, t: (b, 0, 0, t, 0))
    out_shape = (
        jax.ShapeDtypeStruct((B, T, 512), F32),
        jax.ShapeDtypeStruct((B, T, 512), F32),
        jax.ShapeDtypeStruct((B, T, 256), F32),
        jax.ShapeDtypeStruct((B, 6, T, LANES), F32),
        jax.ShapeDtypeStruct((B, 2, T, LANES), F32),
        jax.ShapeDtypeStruct((B, T, LANES), F32),
        jax.ShapeDtypeStruct((B, T, LANES), BF16),
        jax.ShapeDtypeStruct((B, T, 2 * D_MODEL), F32),
        jax.ShapeDtypeStruct((B, 4, A_GROUPS, T, LANES), BF16),
        jax.ShapeDtypeStruct((B, 2, B_KV_HEADS, T, LANES), BF16),
    )
    out_specs = (tok(512), tok(512), tok(256), stack(6), stack(2), tok(LANES), tok(LANES),
                 tok(2 * D_MODEL), att(4), att(2))
    return pl.pallas_call(
        _proj_kernel, grid=(B, nt), out_shape=out_shape, out_specs=out_specs,
        in_specs=[tok(D), const((1, D)), const((D, D_PROJ)), const((8, LANES)), const((LANES, LANES)),
                  tab, tab, tab],
        compiler_params=_cparams(("parallel", "parallel")), name="proj",
    )(x, norm1_g.reshape(1, D), wp, gains, bd, rc, rs1, rs2)


def _compress_kernel(k_ref, v_ref, wk_ref, wv_ref, pek_ref, pev_ref, gkc_ref, kc_ref, vc_ref, *, nch):
    def one(src_ref, w_ref, pe_ref, out_ref, gain):
        r = None
        for j in range(CMP_STRIDE):
            d = _dot(src_ref[pl.ds(j, nch, stride=CMP_STRIDE), :].astype(BF16), w_ref[j * LANES:(j + 1) * LANES, :])
            r = d if r is None else r + d
        rpe = _dot(pe_ref[...].astype(BF16), w_ref[...])
        for g in range(A_GROUPS):
            lo = r[:, g * LANES:(g + 1) * LANES]
            hi = r[:, (2 + g) * LANES:(3 + g) * LANES]
            c = rpe[0:1, g * LANES:(g + 1) * LANES] + rpe[1:2, (2 + g) * LANES:(3 + g) * LANES]
            y = lo + pltpu.roll(hi, nch - 1, 0) + c
            if gain is not None:
                ms = jnp.sum(y * y, axis=-1, keepdims=True) * (1.0 / HEAD_DIM)
                y = y * lax.rsqrt(ms + EPS) * gain
            out_ref[0, g] = y.astype(BF16)

    one(k_ref, wk_ref, pek_ref, kc_ref, gkc_ref[...])
    one(v_ref, wv_ref, pev_ref, vc_ref, None)


def _compress_weights(w_c, pe):
    half = CMP_STRIDE
    w2 = jnp.zeros((half, A_GROUPS, HEAD_DIM, 4, LANES), F32)
    for g in range(A_GROUPS):
        w2 = w2.at[:, g, :, g, :HEAD_DIM].set(w_c[:half])
        w2 = w2.at[:, g, :, 2 + g, :HEAD_DIM].set(w_c[half:])
    w2 = w2.reshape(half * A_GROUPS * HEAD_DIM, 4 * LANES).astype(BF16)
    pe_lo = jnp.tile(pe[:half, None, :], (1, A_GROUPS, 1)).reshape(1, -1)
    pe_hi = jnp.tile(pe[half:, None, :], (1, A_GROUPS, 1)).reshape(1, -1)
    pe2 = jnp.concatenate([pe_lo, pe_hi, jnp.zeros((6, pe_lo.shape[1]), F32)], axis=0)
    return w2, pe2


def _compress(k_src, v_src, nch, wk2, wv2, pek2, pev2, g_kc):
    B = k_src[0].shape[0]
    gkc = jnp.concatenate([g_kc, jnp.zeros((LANES - HEAD_DIM,), F32)]).reshape(1, LANES)

    def src(a, i):
        if a.ndim == 4:
            return pl.BlockSpec((None, None, nch * CMP_STRIDE, LANES), lambda b: (b, i, 0, 0))
        return pl.BlockSpec((None, nch * CMP_STRIDE, LANES), lambda b: (b, 0, 0))
    const = lambda shape: pl.BlockSpec(shape, lambda b: (0,) * len(shape))
    out = pl.BlockSpec((1, A_GROUPS, nch, LANES), lambda b: (b, 0, 0, 0))
    shp = jax.ShapeDtypeStruct((B, A_GROUPS, nch, LANES), BF16)
    return pl.pallas_call(
        functools.partial(_compress_kernel, nch=nch), grid=(B,), out_shape=(shp, shp), out_specs=(out, out),
        in_specs=[src(*k_src), src(*v_src), const(wk2.shape), const(wv2.shape), const(pek2.shape),
                  const(pev2.shape), const((1, LANES))],
        compiler_params=_cparams(("parallel",)), name="compress",
    )(k_src[0], v_src[0], wk2, wv2, pek2, pev2, gkc)


def _heads_rows(q, n_chunks):
    rows = []
    for c in range(n_chunks):
        ch = q[:, c * LANES:(c + 1) * LANES]
        rows.append(ch.astype(BF16))
        rows.append(pltpu.roll(ch, HEAD_DIM, 1).astype(BF16))
    return jnp.concatenate(rows, axis=0)


def _pairs(o, t, n_chunks):
    return [o[(2 * c) * t:(2 * c + 1) * t] + pltpu.roll(o[(2 * c + 1) * t:(2 * c + 2) * t], HEAD_DIM, 1)
            for c in range(n_chunks)]


def _kth_largest(count_ge, k, shape, bits=1):
    def body(i, t):
        step = lax.shift_left(jnp.int32(1), jnp.int32(32 - bits) - bits * i)
        inc = jnp.zeros(shape, I32)
        for r in range(1, 1 << bits):
            inc = inc + jnp.where(count_ge(t + step * r) >= k, 1, 0)
        return t + inc * step
    return lax.fori_loop(0, 32 // bits, body, jnp.full(shape, jnp.iinfo(jnp.int32).min, I32))


def _softmax_rows(s):
    m = jnp.max(s, axis=-1, keepdims=True)
    e = jnp.exp(s - m)
    return e / jnp.sum(e, axis=-1, keepdims=True)


def _nsa_kernel(q_ref, misc_ref, kc_ref, vc_ref, ks_ref, vs_ref, kw_ref, vw_ref, m_ref, tri_ref, eg_ref,
                o_ref, selb_ref, *, tq, nt, pos0, n_cmp, n_slc):
    nch = kc_ref.shape[0]
    nsp = m_ref.shape[1]
    ncs = nsp // LANES
    t0 = pos0 + pl.program_id(2) * tq
    pos = t0 + _row((tq, 1))
    pos4 = jnp.concatenate([pos] * 4, axis=0)
    q4 = _heads_rows(q_ref[...], 2)
    ones = jnp.ones((LANES, LANES), BF16)

    s = _dot_nt(q4, kc_ref[...])
    blk_c = _lane((4 * tq, nch))
    s = jnp.where((blk_c * CMP_STRIDE + (CMP_LEN - 1) <= pos4) & (blk_c < n_cmp), s, NEG_INF)
    p = jnp.where(pos4 >= CMP_LEN - 1, _softmax_rows(s), 0.0)
    o_cmp = _dot(p.astype(BF16), vc_ref[...])
    psum = p[0:tq] + p[tq:2 * tq] + p[2 * tq:3 * tq] + p[3 * tq:4 * tq]
    imp = _split_dot(psum, m_ref[...])

    blk = _lane((tq, nsp))
    cur = pos >> 6
    forced = (blk == 0) | (blk == cur) | (blk == cur - 1)
    adm = (blk * SLC_BLOCK <= pos) & (blk < n_slc)
    key = _order_key(jnp.where(adm, imp + jnp.where(forced, FORCE_BONUS, 0.0), NEG_INF))
    keys = [key[:, c * LANES:(c + 1) * LANES] for c in range(ncs)]

    def count(pred):
        return sum(_dot(jnp.where(pred(kc), 1.0, 0.0).astype(BF16), ones) for kc in keys)

    thr = _kth_largest(lambda cand: count(lambda kc: kc >= cand), float(SLC_TOP), (tq, LANES), bits=4)
    need = float(SLC_TOP) - count(lambda kc: kc > thr)
    thr_w = jnp.concatenate([thr] * ncs, axis=1)
    need_w = jnp.concatenate([need] * ncs, axis=1)
    eq = key == thr_w
    before = _dot(jnp.where(eq, 1.0, 0.0).astype(BF16), tri_ref[...])
    sel = ((key > thr_w) | (eq & (before < need_w))) & adm
    selb = jnp.where(sel, 0.0, NEG_INF)
    for c in range(ncs):
        selb_ref[c] = selb[:, c * LANES:(c + 1) * LANES]

    klast = t0 // KEY_TILE
    blocks_per_tile = KEY_TILE // SLC_BLOCK
    tiles_per_chunk = LANES // blocks_per_tile

    def step(kt, carry, causal):
        m, l, acc = carry
        sb = selb_ref[kt // tiles_per_chunk].astype(BF16)
        qa = jnp.concatenate([q4, jnp.concatenate([sb] * 4, axis=0)], axis=1)
        off = pl.multiple_of(kt * KEY_TILE, KEY_TILE)
        kblk = (kt % tiles_per_chunk) * blocks_per_tile + (_row((KEY_TILE, LANES)) >> 6)
        et = jnp.where(kblk == _lane((KEY_TILE, LANES)), 1.0, 0.0).astype(BF16)
        ka = jnp.concatenate([ks_ref[pl.ds(off, KEY_TILE), :], et], axis=1)
        sc = _dot_nt(qa, ka)
        if causal:
            sc = jnp.where(off + _lane(sc.shape) <= pos4, sc, NEG_INF)
        m_new = jnp.maximum(m, jnp.max(sc, axis=-1, keepdims=True))
        alpha = jnp.exp(m - m_new)
        pe = jnp.exp(sc - m_new)
        l = alpha * l + jnp.sum(pe, axis=-1, keepdims=True)
        acc = alpha * acc + _dot(pe.astype(BF16), vs_ref[pl.ds(off, KEY_TILE), :])
        return m_new, l, acc

    init = (jnp.full((4 * tq, 1), NEG_INF, F32), jnp.zeros((4 * tq, 1), F32), jnp.zeros((4 * tq, LANES), F32))
    carry = lax.fori_loop(0, klast, lambda kt, c: step(kt, c, False), init)
    _, l, acc = step(klast, carry, True)
    o_slc = acc / l

    wl = kw_ref.shape[0] if nt == 1 else WINDOW + tq
    if nt == 1:
        kw, vw = kw_ref[...], vw_ref[...]
    else:
        ws = pl.multiple_of(pl.program_id(2) * tq, tq)
        kw, vw = kw_ref[pl.ds(ws, wl), :], vw_ref[pl.ds(ws, wl), :]
    sw = _dot_nt(q4, kw)
    pos_w = t0 - WINDOW + _lane(sw.shape)
    rel = pos4 - pos_w
    sw = jnp.where((rel >= 0) & (rel <= WINDOW) & (pos_w >= 0), sw, NEG_INF)
    o_win = _dot(_softmax_rows(sw).astype(BF16), vw)

    gates = _split_dot(misc_ref[...], eg_ref[...])
    pc, ps, pw = _pairs(o_cmp, tq, 2), _pairs(o_slc, tq, 2), _pairs(o_win, tq, 2)
    for c in range(2):
        g = lambda br: gates[:, (2 * br + c) * LANES:(2 * br + c + 1) * LANES]
        o_ref[:, c * LANES:(c + 1) * LANES] = g(0) * pc[c] + g(1) * ps[c] + g(2) * pw[c]


def _cmp_to_slc(nch, nsp, n_cmp, n_slc):
    i = np.arange(nch)[:, None]
    j = np.arange(nsp)[None, :]
    start = i * CMP_STRIDE
    ov = np.minimum(start + CMP_LEN, (j + 1) * SLC_BLOCK) - np.maximum(start, j * SLC_BLOCK)
    m = np.clip(ov, 0, None) / CMP_LEN
    m = np.where((i < n_cmp) & (j < n_slc), m, 0.0)
    return jnp.asarray(m, BF16)


def _gate_expand():
    e = np.zeros((A_GROUPS, LANES, 6 * LANES), np.float32)
    for g in range(A_GROUPS):
        for br in range(3):
            for c in range(2):
                for half in range(2):
                    h = 4 * g + 2 * c + half
                    col = (2 * br + c) * LANES + half * HEAD_DIM
                    e[g, MISC_GATE + br * A_HEADS + h, col:col + HEAD_DIM] = 1.0
    return jnp.asarray(e, BF16)


def _nsa(aq, misc, kc, vc, att, kw, vw, *, tq, pos0, n_cmp, n_slc):
    B, T, _ = aq.shape
    nch = kc.shape[2]
    Lk = att.shape[3]
    nsp = -(-max(n_slc, Lk // SLC_BLOCK) // LANES) * LANES
    m = _cmp_to_slc(nch, nsp, n_cmp, n_slc)
    tri = jnp.asarray(np.triu(np.ones((nsp, nsp)), 1), BF16)
    eg = _gate_expand()
    Lw = kw.shape[2]
    nt = T // tq
    in_specs = [
        pl.BlockSpec((None, tq, 256), lambda b, g, t: (b, t, g)),
        pl.BlockSpec((None, tq, LANES), lambda b, g, t: (b, t, 0)),
        _single((None, None, nch, LANES), lambda b, g, t: (b, g, 0, 0)),
        _single((None, None, nch, LANES), lambda b, g, t: (b, g, 0, 0)),
        _single((None, None, None, Lk, LANES), lambda b, g, t: (b, 0, g, 0, 0)),
        _single((None, None, None, Lk, LANES), lambda b, g, t: (b, 1, g, 0, 0)),
        _single((None, None, Lw, LANES), lambda b, g, t: (b, g, 0, 0)),
        _single((None, None, Lw, LANES), lambda b, g, t: (b, g, 0, 0)),
        _single(m.shape, lambda b, g, t: (0, 0)),
        _single(tri.shape, lambda b, g, t: (0, 0)),
        _single((None, LANES, 6 * LANES), lambda b, g, t: (g, 0, 0)),
    ]
    kern = functools.partial(_nsa_kernel, tq=tq, nt=nt, pos0=pos0, n_cmp=n_cmp, n_slc=n_slc)
    return pl.pallas_call(
        kern, grid=(B, A_GROUPS, nt), in_specs=in_specs,
        out_specs=pl.BlockSpec((None, tq, 256), lambda b, g, t: (b, t, g)),
        out_shape=jax.ShapeDtypeStruct((B, T, 512), F32),
        scratch_shapes=[pltpu.VMEM((nsp // LANES, tq, LANES), F32)],
        compiler_params=_cparams(("parallel", "parallel", "arbitrary")), name="nsa",
    )(aq, misc, kc, vc, att, att, kw, vw, m, tri, eg)


def _dsa_kernel(iq_ref, bq_ref, misc_ref, ik_ref, bk_ref, bv_ref, tri_ref, o_ref, key_ref, *, tq, pos0, k_sel):
    t0 = pos0 + pl.program_id(1) * tq
    pos = t0 + _row((tq, 1))
    nkt = t0 // KEY_TILE + 1
    reps = KEY_TILE // LANES
    qi4 = _heads_rows(iq_ref[...], 2)
    misc = misc_ref[...]
    wts = [misc[:, MISC_IW + h:MISC_IW + h + 1] for h in range(4)]
    ones = jnp.ones((LANES, LANES), BF16)
    lane_t = _lane((tq, KEY_TILE))

    def score_step(kt, _):
        off = pl.multiple_of(kt * KEY_TILE, KEY_TILE)
        d = jnp.maximum(_dot_nt(qi4, ik_ref[pl.ds(off, KEY_TILE), :]), 0.0)
        sc = d[0:tq] * wts[0] + d[tq:2 * tq] * wts[1] + d[2 * tq:3 * tq] * wts[2] + d[3 * tq:4 * tq] * wts[3]
        key_ref[kt] = _order_key(jnp.where(off + lane_t <= pos, sc, NEG_INF))
        return 0

    lax.fori_loop(0, nkt, score_step, 0)

    def count(pred, ref_val):
        wide = jnp.concatenate([ref_val] * reps, axis=1)

        def body(kt, c):
            hit = jnp.where(pred(key_ref[kt], wide), 1.0, 0.0)
            for r in range(reps):
                c = c + hit[:, r * LANES:(r + 1) * LANES]
            return c
        part = lax.fori_loop(0, nkt, body, jnp.zeros((tq, LANES), F32))
        return _split_dot(part, ones)

    thr = _kth_largest(lambda cand: count(lambda k, w: k >= w, cand), float(k_sel), (tq, LANES))
    need = float(k_sel) - count(lambda k, w: k > w, thr)
    thr_w = jnp.concatenate([thr] * reps, axis=1)
    need_w = jnp.concatenate([need] * reps, axis=1)

    q8 = [_heads_rows(bq_ref[:, g * 256:(g + 1) * 256], 2) for g in range(B_KV_HEADS)]
    hg = B_HEADS // B_KV_HEADS
    r4 = hg * tq

    def step(kt, carry):
        ms, ls, accs, seen = carry
        off = pl.multiple_of(kt * KEY_TILE, KEY_TILE)
        key = key_ref[kt]
        eq = key == thr_w
        before = seen + _dot(jnp.where(eq, 1.0, 0.0).astype(BF16), tri_ref[...])
        sel = ((key > thr_w) | (eq & (before < need_w))) & (off + lane_t <= pos)
        bias = jnp.where(sel, 0.0, NEG_INF)[None]
        new_m, new_l, new_acc = [], [], []
        for g in range(B_KV_HEADS):
            sc = _dot_nt(q8[g], bk_ref[g, pl.ds(off, KEY_TILE), :]).reshape(hg, tq, KEY_TILE) + bias
            m_new = jnp.maximum(ms[g], jnp.max(sc, axis=-1, keepdims=True))
            alpha = jnp.exp(ms[g] - m_new)
            pe = jnp.exp(sc - m_new)
            new_m.append(m_new)
            new_l.append(alpha * ls[g] + jnp.sum(pe, axis=-1, keepdims=True))
            pv = _dot(pe.reshape(r4, KEY_TILE).astype(BF16), bv_ref[g, pl.ds(off, KEY_TILE), :])
            new_acc.append(alpha.reshape(r4, 1) * accs[g] + pv)
        seen = seen + jnp.sum(jnp.where(eq, 1.0, 0.0), axis=-1, keepdims=True)
        return tuple(new_m), tuple(new_l), tuple(new_acc), seen

    init = (tuple(jnp.full((hg, tq, 1), NEG_INF, F32) for _ in range(B_KV_HEADS)),
            tuple(jnp.zeros((hg, tq, 1), F32) for _ in range(B_KV_HEADS)),
            tuple(jnp.zeros((r4, LANES), F32) for _ in range(B_KV_HEADS)), jnp.zeros((tq, 1), F32))
    _, ls, accs, _ = lax.fori_loop(0, nkt, step, init)
    o = _pairs(jnp.concatenate([a / l.reshape(r4, 1) for a, l in zip(accs, ls)], axis=0), tq, 4)
    for c in range(4):
        o_ref[:, c * LANES:(c + 1) * LANES] = o[c]


def _dsa(iq, bq, misc, ika, batt, *, tq, pos0, k_sel):
    B, T, _ = bq.shape
    Lk = ika.shape[1]
    nt = T // tq
    tri = jnp.asarray(np.triu(np.ones((KEY_TILE, KEY_TILE)), 1), BF16)
    in_specs = [
        pl.BlockSpec((None, tq, 256), lambda b, t: (b, t, 0)),
        pl.BlockSpec((None, tq, 512), lambda b, t: (b, t, 0)),
        pl.BlockSpec((None, tq, LANES), lambda b, t: (b, t, 0)),
        _single((None, Lk, LANES), lambda b, t: (b, 0, 0)),
        _single((None, None, B_KV_HEADS, Lk, LANES), lambda b, t: (b, 0, 0, 0, 0)),
        _single((None, None, B_KV_HEADS, Lk, LANES), lambda b, t: (b, 1, 0, 0, 0)),
        _single(tri.shape, lambda b, t: (0, 0)),
    ]
    scratch = [pltpu.VMEM((Lk // KEY_TILE, tq, KEY_TILE), I32)]
    kern = functools.partial(_dsa_kernel, tq=tq, pos0=pos0, k_sel=k_sel)
    return pl.pallas_call(
        kern, grid=(B, nt), in_specs=in_specs,
        out_specs=pl.BlockSpec((None, tq, 512), lambda b, t: (b, t, 0)),
        out_shape=jax.ShapeDtypeStruct((B, T, 512), F32), scratch_shapes=scratch,
        compiler_params=_cparams(("parallel", "arbitrary")), name="dsa",
    )(iq, bq, misc, ika, batt, batt, tri)


GATHER_PAGES = 8


def _gather_kernel(pt_ref, cmpk_hbm, cmpv_hbm, slck_hbm, slcv_hbm, bk_hbm, bv_hbm, idx_hbm,
                   akv_ref, bkv_ref, misc_ref,
                   cmpk_ref, cmpv_ref, a_att_ref, b_att_ref, ika_ref,
                   b0, b1, b2, b3, b4, sem, *, nfull, ns):
    b, c = pl.program_id(0), pl.program_id(1)
    rows = GATHER_PAGES * PAGE_SIZE
    bufs = (b0, b1, b2, b3, b4)
    low = _lane((rows, LANES)) < HEAD_DIM

    def emit(xs):
        for i, (ref, s) in enumerate(((a_att_ref, 0), (a_att_ref, 1), (b_att_ref, 0), (b_att_ref, 1))):
            ref[s, 0] = jnp.where(low, xs[i], 0.0).astype(BF16)
            ref[s, 1] = jnp.where(low, pltpu.roll(xs[i], HEAD_DIM, 1), 0.0).astype(BF16)
        ika_ref[...] = jnp.where(low, xs[4], 0.0).astype(BF16)

    @pl.when(c < nfull)
    def _():
        copies = []
        for p in range(GATHER_PAGES):
            page = pt_ref[b, c * GATHER_PAGES + p]
            dst = pl.ds(p * PAGE_SIZE, PAGE_SIZE)
            copies.append(pltpu.make_async_copy(cmpk_hbm.at[page], cmpk_ref.at[0, dst], sem.at[0]))
            copies.append(pltpu.make_async_copy(cmpv_hbm.at[page], cmpv_ref.at[0, dst], sem.at[1]))
            for i, pool in enumerate((slck_hbm, slcv_hbm, bk_hbm, bv_hbm, idx_hbm)):
                copies.append(pltpu.make_async_copy(pool.at[page], bufs[i].at[dst], sem.at[2 + i]))
        for cp in copies:
            cp.start()
        for cp in copies:
            cp.wait()
        xi = b4[...]
        emit([b0[...], b1[...], b2[...], b3[...], jnp.concatenate([xi, jnp.zeros_like(xi)], axis=1)])

    @pl.when(c == nfull)
    def _():
        cmpk_ref[...] = jnp.zeros(cmpk_ref.shape, F32)
        cmpv_ref[...] = jnp.zeros(cmpv_ref.shape, F32)
        fill = jnp.zeros((rows - ns, LANES), F32)
        new = [akv_ref[2], akv_ref[3], bkv_ref[0], bkv_ref[1], misc_ref[...]]
        emit([jnp.concatenate([x, fill], axis=0) for x in new])


def _gather_pages(pools, page_table, kv, bkv, misc):
    DB, npg = page_table.shape
    ns = kv.shape[2]
    assert npg % GATHER_PAGES == 0
    nfull = npg // GATHER_PAGES
    rows = GATHER_PAGES * PAGE_SIZE
    L = (nfull + 1) * rows
    any_spec = pl.BlockSpec(memory_space=pl.ANY)
    new = lambda n: pl.BlockSpec((None, n, ns, LANES), lambda b, c, pt: (b, 0, 0, 0))
    raw = pl.BlockSpec((None, rows, LANES), lambda b, c, pt: (b, c, 0))
    dma_dst = pl.BlockSpec((1, rows, LANES), lambda b, c, pt: (b, c, 0))
    att = pl.BlockSpec((None, 2, A_GROUPS, rows, LANES), lambda b, c, pt: (b, 0, 0, c, 0))
    grid_spec = pltpu.PrefetchScalarGridSpec(
        num_scalar_prefetch=1, grid=(DB, nfull + 1),
        in_specs=[any_spec] * 7 + [new(6), new(2), pl.BlockSpec((None, ns, LANES), lambda b, c, pt: (b, 0, 0))],
        out_specs=[dma_dst, dma_dst, att, att, raw],
        scratch_shapes=[pltpu.VMEM((rows, LANES), F32)] * 4 + [pltpu.VMEM((rows, pools[6].shape[-1]), F32),
                                                                pltpu.SemaphoreType.DMA((7,))])
    out_shape = [jax.ShapeDtypeStruct((DB, L, LANES), F32), jax.ShapeDtypeStruct((DB, L, LANES), F32),
                 jax.ShapeDtypeStruct((DB, 2, A_GROUPS, L, LANES), BF16),
                 jax.ShapeDtypeStruct((DB, 2, B_KV_HEADS, L, LANES), BF16),
                 jax.ShapeDtypeStruct((DB, L, LANES), BF16)]
    return pl.pallas_call(
        functools.partial(_gather_kernel, nfull=nfull, ns=ns), grid_spec=grid_spec, out_shape=out_shape,
        compiler_params=_cparams(("parallel", "arbitrary")), name="gather_pages",
    )(page_table, *pools, kv, bkv, misc)


def _topk_rows(s, k, payload=None):
    n = s.shape[0]
    rows = _row(s.shape)
    vals, idxs = [], []
    for _ in range(k):
        m = jnp.max(s, axis=0, keepdims=True)
        idx = jnp.min(jnp.where(s == m, rows, n), axis=0, keepdims=True)
        hit = rows == idx
        vals.append(m)
        idxs.append(idx if payload is None else jnp.max(jnp.where(hit, payload, -1), axis=0, keepdims=True))
        s = jnp.where(hit, -jnp.inf, s)
    return jnp.concatenate(vals, axis=0), jnp.concatenate(idxs, axis=0)


def _route_kernel(x_ref, oa_ref, ob_ref, gates_ref, woa_ref, wob_ref, wo_ref, g2_ref, wq_ref, sub1_ref, sub2_ref,
                  h_ref, hn_ref, eid_ref, gate_ref):
    gates = gates_ref[...]
    ya = _dot(oa_ref[...].astype(BF16), woa_ref[...])
    yb = _dot(ob_ref[...].astype(BF16), wob_ref[...])
    mix = gates[:, :D_MODEL] * ya + gates[:, D_MODEL:] * yb
    h = x_ref[...] + _dot(mix.astype(BF16), wo_ref[...])
    h_ref[...] = h
    hn = h * lax.rsqrt(jnp.mean(h * h, axis=-1, keepdims=True) + EPS) * g2_ref[...]
    hn_ref[...] = hn
    q = _dot(hn.astype(BF16), wq_ref[...])
    half = PEER_KEY_DIM // 2
    for hd in range(PEER_HEADS):
        q1 = q[:, hd * PEER_KEY_DIM:hd * PEER_KEY_DIM + half].astype(BF16)
        q2 = q[:, hd * PEER_KEY_DIM + half:(hd + 1) * PEER_KEY_DIM].astype(BF16)
        v1, i1 = _topk_rows(_dot_nt(sub1_ref[hd], q1), PEER_TOPK)
        v2, i2 = _topk_rows(_dot_nt(sub2_ref[hd], q2), PEER_TOPK)
        cand = jnp.concatenate([v1[r:r + 1] + v2 for r in range(PEER_TOPK)], axis=0)
        cid = jnp.concatenate([i1[r:r + 1] * PEER_NKEYS + i2 for r in range(PEER_TOPK)], axis=0)
        vt, et = _topk_rows(cand, PEER_TOPK, payload=cid)
        e = jnp.exp(vt - vt[0:1])
        eid_ref[0, hd * PEER_TOPK:(hd + 1) * PEER_TOPK, :] = et
        gate_ref[0, hd * PEER_TOPK:(hd + 1) * PEER_TOPK, :] = e / jnp.sum(e, axis=0, keepdims=True)


def _route(x, oa, ob, gates, woa, wob, wo, g2, wq, sub1, sub2, tm):
    N, D = x.shape
    nt = N // tm
    tok = lambda c: pl.BlockSpec((tm, c), lambda i: (i, 0))
    const = lambda a: pl.BlockSpec(a.shape, lambda i: (0,) * a.ndim)
    slot = pl.BlockSpec((1, PEER_HEADS * PEER_TOPK, tm), lambda i: (i, 0, 0))
    consts = (woa, wob, wo, g2, wq, sub1, sub2)
    return pl.pallas_call(
        _route_kernel, grid=(nt,),
        in_specs=[tok(D), tok(512), tok(512), tok(2 * D)] + [const(a) for a in consts],
        out_specs=(tok(D), tok(D), slot, slot),
        out_shape=(jax.ShapeDtypeStruct((N, D), F32), jax.ShapeDtypeStruct((N, D), F32),
                   jax.ShapeDtypeStruct((nt, PEER_HEADS * PEER_TOPK, tm), I32),
                   jax.ShapeDtypeStruct((nt, PEER_HEADS * PEER_TOPK, tm), F32)),
        compiler_params=_cparams(("parallel",)), name="route",
    )(x, oa, ob, gates, *consts)


def _pack_table(t):
    E, D = t.shape
    b = lax.bitcast_convert_type(t.astype(BF16), jnp.uint16).astype(jnp.uint32)
    w = b[:, :D // 2] | (b[:, D // 2:] << 16)
    return lax.bitcast_convert_type(w, I32).reshape(E, D // 256, LANES)


def _unpack_row(w):
    return pltpu.bitcast(w << 16, F32), pltpu.bitcast(w & jnp.int32(-65536), F32)


def _load_table(tab_hbm, tab_vmem, sem):
    @pl.when(pl.program_id(0) == 0)
    def _():
        cp = pltpu.make_async_copy(tab_hbm, tab_vmem, sem)
        cp.start()
        cp.wait()


def _peer_act_kernel(eid_ref, x_ref, gate_ref, tab_hbm, w_ref, tab_vmem, sem, *, tt):
    _load_table(tab_hbm, tab_vmem, sem)
    n_slot = gate_ref.shape[1]
    half = x_ref.shape[1] // 2
    lane = _lane((1, tt))

    def token(t, act):
        x = x_ref[t]
        xlo, xhi = x[0:half], x[half:]
        rows = []
        for j in range(n_slot):
            lo, hi = _unpack_row(tab_vmem[eid_ref[j * tt + t]])
            rows.append(jnp.sum(lo * xlo + hi * xhi, axis=0, keepdims=True))
        a = jnp.sum(jnp.concatenate(rows, axis=0), axis=1, keepdims=True)
        return act + a * jnp.where(lane == t, 1.0, 0.0)

    act = lax.fori_loop(0, tt, token, jnp.zeros((n_slot, tt), F32))
    w_ref[0] = gate_ref[0] * (act * (lax.erf(act * (2.0 ** -0.5)) + 1.0) * 0.5)


def _peer_out_kernel(eid_ref, w_ref, h_ref, tab_hbm, y_ref, tab_vmem, sem, *, tt, n_slot):
    _load_table(tab_hbm, tab_vmem, sem)
    half = h_ref.shape[1] // 2

    def token(t, _):
        acc = [jnp.zeros((half, LANES), F32) for _ in range(4)]
        for j in range(n_slot):
            lo, hi = _unpack_row(tab_vmem[eid_ref[j * tt + t]])
            wj = w_ref[j * tt + t]
            acc[2 * (j % 2)] = acc[2 * (j % 2)] + wj * lo
            acc[2 * (j % 2) + 1] = acc[2 * (j % 2) + 1] + wj * hi
        y_ref[t] = h_ref[t] + jnp.concatenate([acc[0] + acc[2], acc[1] + acc[3]], axis=0)
        return 0

    lax.fori_loop(0, tt, token, 0)


def _peer(eid, gate, hn, h, u_tab, v_tab, tt):
    nt, n_slot, _ = eid.shape
    N, D = h.shape
    sub = D // LANES
    eid_flat = eid.reshape(-1)
    smem = lambda: pl.BlockSpec((n_slot * tt,), lambda i: (i,), memory_space=pltpu.SMEM)
    rows = pl.BlockSpec((tt, sub, LANES), lambda i: (i, 0, 0))
    slot = pl.BlockSpec((1, n_slot, tt), lambda i: (i, 0, 0))
    any_spec = pl.BlockSpec(memory_space=pl.ANY)
    scratch = [pltpu.VMEM(u_tab.shape, I32), pltpu.SemaphoreType.DMA(())]
    w = pl.pallas_call(
        functools.partial(_peer_act_kernel, tt=tt), grid=(nt,),
        in_specs=[smem(), rows, slot, any_spec], out_specs=slot,
        out_shape=jax.ShapeDtypeStruct((nt, n_slot, tt), F32), scratch_shapes=scratch,
        compiler_params=_cparams(("arbitrary",)), name="peer_act",
    )(eid_flat, hn.reshape(N, sub, LANES), gate, u_tab)
    y = pl.pallas_call(
        functools.partial(_peer_out_kernel, tt=tt, n_slot=n_slot), grid=(nt,),
        in_specs=[smem(), smem(), rows, any_spec], out_specs=rows,
        out_shape=jax.ShapeDtypeStruct((N, sub, LANES), F32), scratch_shapes=scratch,
        compiler_params=_cparams(("arbitrary",)), name="peer_out",
    )(eid_flat, w.reshape(-1), h.reshape(N, sub, LANES), v_tab)
    return y.reshape(N, D)


def _relayout_w_in(w_in):
    names = (('a_q', 512), ('a_k_cmp', 128), ('a_v_cmp', 128), ('a_k_slc', 128), ('a_v_slc', 128),
             ('a_k_win', 128), ('a_v_win', 128), ('a_gate', 24), ('b_q', 512), ('b_k', 128), ('b_v', 128),
             ('i_q', 256), ('i_w', 4), ('i_k', 64), ('merge', 2 * D_MODEL))
    sec, o = {}, 0
    for name, n in names:
        sec[name] = w_in[:, o:o + n]
        o += n
    pad = jnp.zeros((w_in.shape[0], LANES - 64 - 4 - 24), w_in.dtype)
    order = ['a_q', 'a_k_cmp', 'a_v_cmp', 'a_k_slc', 'a_v_slc', 'a_k_win', 'a_v_win', 'b_q', 'b_k', 'b_v',
             'i_q', 'i_k', 'i_w', 'a_gate']
    return jnp.concatenate([sec[n] for n in order] + [pad, sec['merge']], axis=1).astype(BF16)


def _group_lanes(x):
    z = jnp.zeros_like(x[..., :HEAD_DIM])
    parts = [jnp.concatenate([x[..., g * HEAD_DIM:(g + 1) * HEAD_DIM], z], axis=-1) for g in range(2)]
    return jnp.stack(parts, axis=-3).astype(BF16)


def _pad_rows(x, total, axis):
    pad = [(0, 0)] * x.ndim
    pad[axis] = (0, total - x.shape[axis])
    return jnp.pad(x, pad)


def kernel(x_prompt, x_sample, cache_a_cmp_k, cache_a_cmp_v, cache_a_slc_k, cache_a_slc_v, state_a_win_k, state_a_win_v, cache_b_k, cache_b_v, cache_b_idx_k, page_table, norm1_g, w_in, g_qa, g_ka_cmp, g_ka_slc, g_ka_win, g_qb, g_kb, g_kidx, w_cmp_k, pe_cmp_k, w_cmp_v, pe_cmp_v, g_kc, w_out_a, w_out_b, w_o, norm2_g, w_peer_q, peer_subkeys, peer_u, peer_v):
    B, S, D = x_prompt.shape
    DB, NS, _ = x_sample.shape
    npg = page_table.shape[1]
    past = npg * PAGE_SIZE
    assert D == D_MODEL and S % KEY_TILE == 0 and past % KEY_TILE == 0 and NS < CMP_STRIDE
    assert state_a_win_k.shape[1] == WINDOW

    wp = _relayout_w_in(w_in)
    ones = jnp.ones((HEAD_DIM,), F32)
    gains = jnp.stack([jnp.tile(g, 2) for g in (g_qa, g_ka_cmp, g_ka_slc, g_ka_win, g_qb, g_kb)]
                      + [jnp.concatenate([g_kidx, ones]), jnp.tile(ones, 2)])
    wk2, pek2 = _compress_weights(w_cmp_k, pe_cmp_k)
    wv2, pev2 = _compress_weights(w_cmp_v, pe_cmp_v)
    woa, wob, wo = w_out_a.astype(BF16), w_out_b.astype(BF16), w_o.astype(BF16)
    wq = w_peer_q.reshape(D, PEER_HEADS * PEER_KEY_DIM).astype(BF16)
    sub1, sub2 = peer_subkeys[0].astype(BF16), peer_subkeys[1].astype(BF16)
    g2 = norm2_g.reshape(1, D)
    u_tab, v_tab = _pack_table(peer_u), _pack_table(peer_v)

    def finish(x, oa, ob, gates):
        n = x.shape[0] * x.shape[1]
        tt = 128 if n % 128 == 0 else n
        h, hn, eid, gate = _route(x.reshape(n, D), oa.reshape(n, 512), ob.reshape(n, 512),
                                  gates.reshape(n, 2 * D), woa, wob, wo, g2, wq, sub1, sub2, tt)
        return _peer(eid, gate, hn, h, u_tab, v_tab, tt).reshape(x.shape)

    def heads(a, n):
        return a.reshape(a.shape[0], a.shape[1], n, HEAD_DIM)

    total = past + NS
    aq, bq, iq, kv, bkv, misc, ika, gates, a_att, b_att = _project(
        x_sample, past + jnp.arange(NS), norm1_g, wp, gains, NS)
    flat = lambda c: c.reshape(c.shape[0], PAGE_SIZE, -1)
    cmpk_s, cmpv_s, a_att_s, b_att_s, ika_s = _gather_pages(
        [flat(c) for c in (cache_a_cmp_k, cache_a_cmp_v, cache_a_slc_k, cache_a_slc_v, cache_b_k, cache_b_v,
                           cache_b_idx_k)], page_table, kv, bkv, misc)
    nch_s = total // CMP_STRIDE
    assert nch_s * CMP_STRIDE <= past
    kc, vc = _compress((cmpk_s, 0), (cmpv_s, 0), nch_s, wk2, wv2, pek2, pev2, g_kc)
    wl = -(-(WINDOW + NS) // LANES) * LANES
    kw_s = _pad_rows(jnp.concatenate([_group_lanes(state_a_win_k.reshape(DB, WINDOW, LANES)), a_att[:, 2]], axis=2), wl, 2)
    vw_s = _pad_rows(jnp.concatenate([_group_lanes(state_a_win_v.reshape(DB, WINDOW, LANES)), a_att[:, 3]], axis=2), wl, 2)
    o_a = _nsa(aq, misc, kc, vc, a_att_s, kw_s, vw_s, tq=NS, pos0=past, n_cmp=nch_s - 1,
               n_slc=-(-total // SLC_BLOCK))
    o_b = _dsa(iq, bq, misc, ika_s, b_att_s, tq=NS, pos0=past, k_sel=min(DSA_TOPK_MAX, total // 4))
    y_sample = finish(x_sample, o_a, o_b, gates)
    s_out = [heads(kv[:, i], A_GROUPS) for i in range(4)]
    s_out += [jnp.concatenate([state_a_win_k[:, NS:], heads(kv[:, 4], A_GROUPS)], axis=1),
              jnp.concatenate([state_a_win_v[:, NS:], heads(kv[:, 5], A_GROUPS)], axis=1)]
    s_out += [heads(bkv[:, 0], B_KV_HEADS), heads(bkv[:, 1], B_KV_HEADS), misc[:, :, :IDX_DIM]]

    tq = 128
    aq, bq, iq, kv, bkv, misc, ika, gates, a_att, b_att = _project(
        x_prompt, jnp.arange(S), norm1_g, wp, gains, 256)
    nch = S // CMP_STRIDE
    kc, vc = _compress((kv, 0), (kv, 1), nch, wk2, wv2, pek2, pev2, g_kc)
    front = ((0, 0), (0, 0), (WINDOW, 0), (0, 0))
    o_a = _nsa(aq, misc, kc, vc, a_att, jnp.pad(a_att[:, 2], front), jnp.pad(a_att[:, 3], front),
               tq=tq, pos0=0, n_cmp=nch - 1, n_slc=-(-S // SLC_BLOCK))
    o_b = _dsa(iq, bq, misc, ika, b_att, tq=tq, pos0=0, k_sel=min(DSA_TOPK_MAX, S // 4))
    y_prompt = finish(x_prompt, o_a, o_b, gates)
    win_p = min(WINDOW, S)
    p_out = [heads(kv[:, i], A_GROUPS) for i in range(4)]
    p_out += [heads(kv[:, i, S - win_p:], A_GROUPS) for i in (4, 5)]
    p_out += [heads(bkv[:, 0], B_KV_HEADS), heads(bkv[:, 1], B_KV_HEADS), misc[:, :, :IDX_DIM]]
    return (y_prompt, y_sample, *p_out, *s_out)
```

```python
import functools

import numpy as np
import jax
import jax.numpy as jnp
from jax import lax
from jax.experimental import pallas as pl
from jax.experimental.pallas import tpu as pltpu

F32 = jnp.float32
BF16 = jnp.bfloat16
I32 = jnp.int32

LANES = 128
HEAD_DIM = 64
ROPE_DIM = HEAD_DIM // 4
ROPE_THETA = 500000.0
QK_SCALE = HEAD_DIM ** -0.5
EPS = 1e-6
NEG_INF = -1e30
A_HEADS = 8
A_GROUPS = 2
CMP_STRIDE = 16
CMP_LEN = 32
SLC_BLOCK = 64
SLC_TOP = 16
FORCE_BONUS = 1e4
WINDOW = 512
B_HEADS = 8
B_KV_HEADS = 2
IDX_HEADS = 4
IDX_DIM = 64
IDX_SCALE = (IDX_HEADS * IDX_DIM) ** -0.5
DSA_TOPK_MAX = 256
PEER_HEADS = 8
PEER_NKEYS = 128
PEER_KEY_DIM = 256
PEER_TOPK = 16
PAGE_SIZE = 128
D_MODEL = 1024

KEY_TILE = 512
VMEM_LIMIT = 56 * 1024 * 1024

OFF_AQ, OFF_KV, OFF_BQ, OFF_BKV, OFF_IQ, OFF_MISC, OFF_MERGE = 0, 512, 1280, 1792, 2048, 2304, 2432
D_PROJ = OFF_MERGE + 2 * D_MODEL
MISC_IW = 64
MISC_GATE = 68

_NT = (((1,), (1,)), ((), ()))


def _dot(a, b):
    return jnp.dot(a, b, preferred_element_type=F32)


def _dot_nt(a, b):
    return lax.dot_general(a, b, _NT, preferred_element_type=F32)


def _split_dot(x, w):
    hi = x.astype(BF16)
    lo = (x - hi.astype(F32)).astype(BF16)
    return _dot(hi, w) + _dot(lo, w)


def _lane(shape):
    return lax.broadcasted_iota(I32, shape, len(shape) - 1)


def _row(shape):
    return lax.broadcasted_iota(I32, shape, len(shape) - 2)


def _order_key(x):
    b = pltpu.bitcast(x + 0.0, I32)
    return b ^ ((b >> 31) & jnp.int32(0x7FFFFFFF))


def _cparams(sem):
    return pltpu.CompilerParams(dimension_semantics=sem, vmem_limit_bytes=VMEM_LIMIT)


def _single(shape, index_map):
    return pl.BlockSpec(shape, index_map, pipeline_mode=pl.Buffered(1))


def _proj_kernel(x_ref, g1_ref, w_ref, gains_ref, bd_ref, rc_ref, rs1_ref, rs2_ref,
                 aq_ref, bq_ref, iq_ref, kv_ref, bkv_ref, misc_ref, ika_ref, gates_ref,
                 akv_att_ref, bkv_att_ref):
    x = x_ref[0]
    xn = x * lax.rsqrt(jnp.mean(x * x, axis=-1, keepdims=True) + EPS) * g1_ref[...]
    xb = xn.astype(BF16)
    bd = bd_ref[...]
    rc, rs1, rs2 = rc_ref[...], rs1_ref[...], rs2_ref[...]
    lane = _lane((x.shape[0], LANES))
    low = lane < HEAD_DIM

    def proj(a, b):
        return _dot(xb, w_ref[:, a:b])

    def norm(y, g):
        return y * lax.rsqrt(_split_dot(y * y, bd) * (1.0 / HEAD_DIM) + EPS) * g

    def rope(y):
        return y * rc + pltpu.roll(y, LANES - ROPE_DIM // 2, 1) * rs1 + pltpu.roll(y, ROPE_DIM // 2, 1) * rs2

    def att_copy(ref, s, y):
        ref[0, s, 0] = jnp.where(low, y, 0.0).astype(BF16)
        ref[0, s, 1] = jnp.where(low, pltpu.roll(y, HEAD_DIM, 1), 0.0).astype(BF16)

    z = proj(OFF_AQ, OFF_AQ + 512)
    for c in range(4):
        aq_ref[0, :, c * LANES:(c + 1) * LANES] = rope(norm(z[:, c * LANES:(c + 1) * LANES], gains_ref[0:1])) * QK_SCALE
    z = proj(OFF_KV, OFF_KV + 768)
    for s in range(6):
        y = z[:, s * LANES:(s + 1) * LANES]
        if s % 2 == 0:
            y = rope(norm(y, gains_ref[1 + s // 2:2 + s // 2]))
        kv_ref[0, s] = y
        if s >= 2:
            att_copy(akv_att_ref, s - 2, y)
    z = proj(OFF_BQ, OFF_BQ + 512)
    for c in range(4):
        bq_ref[0, :, c * LANES:(c + 1) * LANES] = rope(norm(z[:, c * LANES:(c + 1) * LANES], gains_ref[4:5])) * QK_SCALE
    z = proj(OFF_BKV, OFF_BKV + 256)
    yk = rope(norm(z[:, :LANES], gains_ref[5:6]))
    yv = z[:, LANES:]
    bkv_ref[0, 0] = yk
    bkv_ref[0, 1] = yv
    att_copy(bkv_att_ref, 0, yk)
    att_copy(bkv_att_ref, 1, yv)
    z = proj(OFF_IQ, OFF_IQ + 256)
    for c in range(2):
        iq_ref[0, :, c * LANES:(c + 1) * LANES] = rope(z[:, c * LANES:(c + 1) * LANES])
    z = proj(OFF_MISC, OFF_MISC + LANES)
    ik = rope(norm(z, gains_ref[6:7]))
    misc_ref[0] = jnp.where(low, ik, jnp.where(lane < MISC_GATE, z * IDX_SCALE, jax.nn.sigmoid(z)))
    ika_ref[0] = jnp.where(low, ik, 0.0).astype(BF16)
    gates_ref[0] = jax.nn.sigmoid(proj(OFF_MERGE, OFF_MERGE + 2 * D_MODEL))


def _project(x, pos, norm1_g, wp, gains, tm):
    B, T, D = x.shape
    half = ROPE_DIM // 2
    inv = jnp.power(ROPE_THETA, -jnp.arange(half, dtype=F32) / half)
    ang = pos.astype(F32)[:, None] * inv[None, :]
    cos, sin = jnp.cos(ang), jnp.sin(ang)
    one = jnp.ones((T, HEAD_DIM - ROPE_DIM), F32)
    zero = jnp.zeros((T, HEAD_DIM - ROPE_DIM), F32)
    zh = jnp.zeros((T, half), F32)
    rc = jnp.tile(jnp.concatenate([cos, cos, one], axis=1), (1, 2))
    rs1 = jnp.tile(jnp.concatenate([-sin, zh, zero], axis=1), (1, 2))
    rs2 = jnp.tile(jnp.concatenate([zh, sin, zero], axis=1), (1, 2))
    bd = jnp.asarray(np.kron(np.eye(2), np.ones((HEAD_DIM, HEAD_DIM))), BF16)
    nt = T // tm
    tok = lambda c: pl.BlockSpec((1, tm, c), lambda b, t: (b, t, 0))
    const = lambda shape: pl.BlockSpec(shape, lambda b, t: (0,) * len(shape))
    tab = pl.BlockSpec((tm, LANES), lambda b, t: (t, 0))
    stack = lambda n: pl.BlockSpec((1, n, tm, LANES), lambda b, t: (b, 0, t, 0))
    att = lambda n: pl.BlockSpec((1, n, A_GROUPS, tm, LANES), lambda b, t: (b, 0, 0, t, 0))
    out_shape = (
        jax.ShapeDtypeStruct((B, T, 512), F32),
        jax.ShapeDtypeStruct((B, T, 512), F32),
        jax.ShapeDtypeStruct((B, T, 256), F32),
        jax.ShapeDtypeStruct((B, 6, T, LANES), F32),
        jax.ShapeDtypeStruct((B, 2, T, LANES), F32),
        jax.ShapeDtypeStruct((B, T, LANES), F32),
        jax.ShapeDtypeStruct((B, T, LANES), BF16),
        jax.ShapeDtypeStruct((B, T, 2 * D_MODEL), F32),
        jax.ShapeDtypeStruct((B, 4, A_GROUPS, T, LANES), BF16),
        jax.ShapeDtypeStruct((B, 2, B_KV_HEADS, T, LANES), BF16),
    )
    out_specs = (tok(512), tok(512), tok(256), stack(6), stack(2), tok(LANES), tok(LANES),
                 tok(2 * D_MODEL), att(4), att(2))
    return pl.pallas_call(
        _proj_kernel, grid=(B, nt), out_shape=out_shape, out_specs=out_specs,
        in_specs=[tok(D), const((1, D)), const((D, D_PROJ)), const((8, LANES)), const((LANES, LANES)),
                  tab, tab, tab],
        compiler_params=_cparams(("parallel", "parallel")), name="proj",
    )(x, norm1_g.reshape(1, D), wp, gains, bd, rc, rs1, rs2)


def _compress_kernel(k_ref, v_ref, wk_ref, wv_ref, pek_ref, pev_ref, gkc_ref, kc_ref, vc_ref, *, nch):
    def one(src_ref, w_ref, pe_ref, out_ref, gain):
        r = None
        for j in range(CMP_STRIDE):
            d = _dot(src_ref[pl.ds(j, nch, stride=CMP_STRIDE), :].astype(BF16), w_ref[j * LANES:(j + 1) * LANES, :])
            r = d if r is None else r + d
        rpe = _dot(pe_ref[...].astype(BF16), w_ref[...])
        for g in range(A_GROUPS):
            lo = r[:, g * LANES:(g + 1) * LANES]
            hi = r[:, (2 + g) * LANES:(3 + g) * LANES]
            c = rpe[0:1, g * LANES:(g + 1) * LANES] + rpe[1:2, (2 + g) * LANES:(3 + g) * LANES]
            y = lo + pltpu.roll(hi, nch - 1, 0) + c
            if gain is not None:
                ms = jnp.sum(y * y, axis=-1, keepdims=True) * (1.0 / HEAD_DIM)
                y = y * lax.rsqrt(ms + EPS) * gain
            out_ref[0, g] = y.astype(BF16)

    one(k_ref, wk_ref, pek_ref, kc_ref, gkc_ref[...])
    one(v_ref, wv_ref, pev_ref, vc_ref, None)


def _compress_weights(w_c, pe):
    half = CMP_STRIDE
    w2 = jnp.zeros((half, A_GROUPS, HEAD_DIM, 4, LANES), F32)
    for g in range(A_GROUPS):
        w2 = w2.at[:, g, :, g, :HEAD_DIM].set(w_c[:half])
        w2 = w2.at[:, g, :, 2 + g, :HEAD_DIM].set(w_c[half:])
    w2 = w2.reshape(half * A_GROUPS * HEAD_DIM, 4 * LANES).astype(BF16)
    pe_lo = jnp.tile(pe[:half, None, :], (1, A_GROUPS, 1)).reshape(1, -1)
    pe_hi = jnp.tile(pe[half:, None, :], (1, A_GROUPS, 1)).reshape(1, -1)
    pe2 = jnp.concatenate([pe_lo, pe_hi, jnp.zeros((6, pe_lo.shape[1]), F32)], axis=0)
    return w2, pe2


def _compress(k_src, v_src, nch, wk2, wv2, pek2, pev2, g_kc):
    B = k_src[0].shape[0]
    gkc = jnp.concatenate([g_kc, jnp.zeros((LANES - HEAD_DIM,), F32)]).reshape(1, LANES)

    def src(a, i):
        if a.ndim == 4:
            return pl.BlockSpec((None, None, nch * CMP_STRIDE, LANES), lambda b: (b, i, 0, 0))
        return pl.BlockSpec((None, nch * CMP_STRIDE, LANES), lambda b: (b, 0, 0))
    const = lambda shape: pl.BlockSpec(shape, lambda b: (0,) * len(shape))
    out = pl.BlockSpec((1, A_GROUPS, nch, LANES), lambda b: (b, 0, 0, 0))
    shp = jax.ShapeDtypeStruct((B, A_GROUPS, nch, LANES), BF16)
    return pl.pallas_call(
        functools.partial(_compress_kernel, nch=nch), grid=(B,), out_shape=(shp, shp), out_specs=(out, out),
        in_specs=[src(*k_src), src(*v_src), const(wk2.shape), const(wv2.shape), const(pek2.shape),
                  const(pev2.shape), const((1, LANES))],
        compiler_params=_cparams(("parallel",)), name="compress",
    )(k_src[0], v_src[0], wk2, wv2, pek2, pev2, gkc)


def _heads_rows(q, n_chunks):
    rows = []
    for c in range(n_chunks):
        ch = q[:, c * LANES:(c + 1) * LANES]
        rows.append(ch.astype(BF16))
        rows.append(pltpu.roll(ch, HEAD_DIM, 1).astype(BF16))
    return jnp.concatenate(rows, axis=0)


def _pairs(o, t, n_chunks):
    return [o[(2 * c) * t:(2 * c + 1) * t] + pltpu.roll(o[(2 * c + 1) * t:(2 * c + 2) * t], HEAD_DIM, 1)
            for c in range(n_chunks)]


def _kth_largest(count_ge, k, shape, bits=1):
    def body(i, t):
        step = lax.shift_left(jnp.int32(1), jnp.int32(32 - bits) - bits * i)
        inc = jnp.zeros(shape, I32)
        for r in range(1, 1 << bits):
            inc = inc + jnp.where(count_ge(t + step * r) >= k, 1, 0)
        return t + inc * step
    return lax.fori_loop(0, 32 // bits, body, jnp.full(shape, jnp.iinfo(jnp.int32).min, I32))


def _softmax_rows(s):
    m = jnp.max(s, axis=-1, keepdims=True)
    e = jnp.exp(s - m)
    return e / jnp.sum(e, axis=-1, keepdims=True)


def _nsa_kernel(q_ref, misc_ref, kc_ref, vc_ref, ks_ref, vs_ref, kw_ref, vw_ref, m_ref, tri_ref, eg_ref,
                o_ref, selb_ref, *, tq, nt, pos0, n_cmp, n_slc):
    nch = kc_ref.shape[0]
    nsp = m_ref.shape[1]
    ncs = nsp // LANES
    t0 = pos0 + pl.program_id(2) * tq
    pos = t0 + _row((tq, 1))
    pos4 = jnp.concatenate([pos] * 4, axis=0)
    q4 = _heads_rows(q_ref[...], 2)
    ones = jnp.ones((LANES, LANES), BF16)

    s = _dot_nt(q4, kc_ref[...])
    blk_c = _lane((4 * tq, nch))
    s = jnp.where((blk_c * CMP_STRIDE + (CMP_LEN - 1) <= pos4) & (blk_c < n_cmp), s, NEG_INF)
    p = jnp.where(pos4 >= CMP_LEN - 1, _softmax_rows(s), 0.0)
    o_cmp = _dot(p.astype(BF16), vc_ref[...])
    psum = p[0:tq] + p[tq:2 * tq] + p[2 * tq:3 * tq] + p[3 * tq:4 * tq]
    imp = _split_dot(psum, m_ref[...])

    blk = _lane((tq, nsp))
    cur = pos >> 6
    forced = (blk == 0) | (blk == cur) | (blk == cur - 1)
    adm = (blk * SLC_BLOCK <= pos) & (blk < n_slc)
    key = _order_key(jnp.where(adm, imp + jnp.where(forced, FORCE_BONUS, 0.0), NEG_INF))
    keys = [key[:, c * LANES:(c + 1) * LANES] for c in range(ncs)]

    def count(pred):
        return sum(_dot(jnp.where(pred(kc), 1.0, 0.0).astype(BF16), ones) for kc in keys)

    thr = _kth_largest(lambda cand: count(lambda kc: kc >= cand), float(SLC_TOP), (tq, LANES), bits=4)
    need = float(SLC_TOP) - count(lambda kc: kc > thr)
    thr_w = jnp.concatenate([thr] * ncs, axis=1)
    need_w = jnp.concatenate([need] * ncs, axis=1)
    eq = key == thr_w
    before = _dot(jnp.where(eq, 1.0, 0.0).astype(BF16), tri_ref[...])
    sel = ((key > thr_w) | (eq & (before < need_w))) & adm
    selb = jnp.where(sel, 0.0, NEG_INF)
    for c in range(ncs):
        selb_ref[c] = selb[:, c * LANES:(c + 1) * LANES]

    klast = t0 // KEY_TILE
    blocks_per_tile = KEY_TILE // SLC_BLOCK
    tiles_per_chunk = LANES // blocks_per_tile

    def step(kt, carry, causal):
        m, l, acc = carry
        sb = selb_ref[kt // tiles_per_chunk].astype(BF16)
        qa = jnp.concatenate([q4, jnp.concatenate([sb] * 4, axis=0)], axis=1)
        off = pl.multiple_of(kt * KEY_TILE, KEY_TILE)
        kblk = (kt % tiles_per_chunk) * blocks_per_tile + (_row((KEY_TILE, LANES)) >> 6)
        et = jnp.where(kblk == _lane((KEY_TILE, LANES)), 1.0, 0.0).astype(BF16)
        ka = jnp.concatenate([ks_ref[pl.ds(off, KEY_TILE), :], et], axis=1)
        sc = _dot_nt(qa, ka)
        if causal:
            sc = jnp.where(off + _lane(sc.shape) <= pos4, sc, NEG_INF)
        m_new = jnp.maximum(m, jnp.max(sc, axis=-1, keepdims=True))
        alpha = jnp.exp(m - m_new)
        pe = jnp.exp(sc - m_new)
        l = alpha * l + jnp.sum(pe, axis=-1, keepdims=True)
        acc = alpha * acc + _dot(pe.astype(BF16), vs_ref[pl.ds(off, KEY_TILE), :])
        return m_new, l, acc

    init = (jnp.full((4 * tq, 1), NEG_INF, F32), jnp.zeros((4 * tq, 1), F32), jnp.zeros((4 * tq, LANES), F32))
    carry = lax.fori_loop(0, klast, lambda kt, c: step(kt, c, False), init)
    _, l, acc = step(klast, carry, True)
    o_slc = acc / l

    wl = kw_ref.shape[0] if nt == 1 else WINDOW + tq
    if nt == 1:
        kw, vw = kw_ref[...], vw_ref[...]
    else:
        ws = pl.multiple_of(pl.program_id(2) * tq, tq)
        kw, vw = kw_ref[pl.ds(ws, wl), :], vw_ref[pl.ds(ws, wl), :]
    sw = _dot_nt(q4, kw)
    pos_w = t0 - WINDOW + _lane(sw.shape)
    rel = pos4 - pos_w
    sw = jnp.where((rel >= 0) & (rel <= WINDOW) & (pos_w >= 0), sw, NEG_INF)
    o_win = _dot(_softmax_rows(sw).astype(BF16), vw)

    gates = _split_dot(misc_ref[...], eg_ref[...])
    pc, ps, pw = _pairs(o_cmp, tq, 2), _pairs(o_slc, tq, 2), _pairs(o_win, tq, 2)
    for c in range(2):
        g = lambda br: gates[:, (2 * br + c) * LANES:(2 * br + c + 1) * LANES]
        o_ref[:, c * LANES:(c + 1) * LANES] = g(0) * pc[c] + g(1) * ps[c] + g(2) * pw[c]


def _cmp_to_slc(nch, nsp, n_cmp, n_slc):
    i = np.arange(nch)[:, None]
    j = np.arange(nsp)[None, :]
    start = i * CMP_STRIDE
    ov = np.minimum(start + CMP_LEN, (j + 1) * SLC_BLOCK) - np.maximum(start, j * SLC_BLOCK)
    m = np.clip(ov, 0, None) / CMP_LEN
    m = np.where((i < n_cmp) & (j < n_slc), m, 0.0)
    return jnp.asarray(m, BF16)


def _gate_expand():
    e = np.zeros((A_GROUPS, LANES, 6 * LANES), np.float32)
    for g in range(A_GROUPS):
        for br in range(3):
            for c in range(2):
                for half in range(2):
                    h = 4 * g + 2 * c + half
                    col = (2 * br + c) * LANES + half * HEAD_DIM
                    e[g, MISC_GATE + br * A_HEADS + h, col:col + HEAD_DIM] = 1.0
    return jnp.asarray(e, BF16)


def _nsa(aq, misc, kc, vc, att, kw, vw, *, tq, pos0, n_cmp, n_slc):
    B, T, _ = aq.shape
    nch = kc.shape[2]
    Lk = att.shape[3]
    nsp = -(-max(n_slc, Lk // SLC_BLOCK) // LANES) * LANES
    m = _cmp_to_slc(nch, nsp, n_cmp, n_slc)
    tri = jnp.asarray(np.triu(np.ones((nsp, nsp)), 1), BF16)
    eg = _gate_expand()
    Lw = kw.shape[2]
    nt = T // tq
    in_specs = [
        pl.BlockSpec((None, tq, 256), lambda b, g, t: (b, t, g)),
        pl.BlockSpec((None, tq, LANES), lambda b, g, t: (b, t, 0)),
        _single((None, None, nch, LANES), lambda b, g, t: (b, g, 0, 0)),
        _single((None, None, nch, LANES), lambda b, g, t: (b, g, 0, 0)),
        _single((None, None, None, Lk, LANES), lambda b, g, t: (b, 0, g, 0, 0)),
        _single((None, None, None, Lk, LANES), lambda b, g, t: (b, 1, g, 0, 0)),
        _single((None, None, Lw, LANES), lambda b, g, t: (b, g, 0, 0)),
        _single((None, None, Lw, LANES), lambda b, g, t: (b, g, 0, 0)),
        _single(m.shape, lambda b, g, t: (0, 0)),
        _single(tri.shape, lambda b, g, t: (0, 0)),
        _single((None, LANES, 6 * LANES), lambda b, g, t: (g, 0, 0)),
    ]
    kern = functools.partial(_nsa_kernel, tq=tq, nt=nt, pos0=pos0, n_cmp=n_cmp, n_slc=n_slc)
    return pl.pallas_call(
        kern, grid=(B, A_GROUPS, nt), in_specs=in_specs,
        out_specs=pl.BlockSpec((None, tq, 256), lambda b, g, t: (b, t, g)),
        out_shape=jax.ShapeDtypeStruct((B, T, 512), F32),
        scratch_shapes=[pltpu.VMEM((nsp // LANES, tq, LANES), F32)],
        compiler_params=_cparams(("parallel", "parallel", "arbitrary")), name="nsa",
    )(aq, misc, kc, vc, att, att, kw, vw, m, tri, eg)


def _dsa_kernel(iq_ref, bq_ref, misc_ref, ik_ref, bk_ref, bv_ref, tri_ref, o_ref, key_ref, *, tq, pos0, k_sel):
    t0 = pos0 + pl.program_id(1) * tq
    pos = t0 + _row((tq, 1))
    nkt = t0 // KEY_TILE + 1
    reps = KEY_TILE // LANES
    qi4 = _heads_rows(iq_ref[...], 2)
    misc = misc_ref[...]
    wts = [misc[:, MISC_IW + h:MISC_IW + h + 1] for h in range(4)]
    ones = jnp.ones((LANES, LANES), BF16)
    lane_t = _lane((tq, KEY_TILE))

    def score_step(kt, _):
        off = pl.multiple_of(kt * KEY_TILE, KEY_TILE)
        d = jnp.maximum(_dot_nt(qi4, ik_ref[pl.ds(off, KEY_TILE), :]), 0.0)
        sc = d[0:tq] * wts[0] + d[tq:2 * tq] * wts[1] + d[2 * tq:3 * tq] * wts[2] + d[3 * tq:4 * tq] * wts[3]
        key_ref[kt] = _order_key(jnp.where(off + lane_t <= pos, sc, NEG_INF))
        return 0

    lax.fori_loop(0, nkt, score_step, 0)

    def count(pred, ref_val):
        wide = jnp.concatenate([ref_val] * reps, axis=1)

        def body(kt, c):
            hit = jnp.where(pred(key_ref[kt], wide), 1.0, 0.0)
            for r in range(reps):
                c = c + hit[:, r * LANES:(r + 1) * LANES]
            return c
        part = lax.fori_loop(0, nkt, body, jnp.zeros((tq, LANES), F32))
        return _split_dot(part, ones)

    thr = _kth_largest(lambda cand: count(lambda k, w: k >= w, cand), float(k_sel), (tq, LANES))
    need = float(k_sel) - count(lambda k, w: k > w, thr)
    thr_w = jnp.concatenate([thr] * reps, axis=1)
    need_w = jnp.concatenate([need] * reps, axis=1)

    q8 = [_heads_rows(bq_ref[:, g * 256:(g + 1) * 256], 2) for g in range(B_KV_HEADS)]
    hg = B_HEADS // B_KV_HEADS
    r4 = hg * tq

    def step(kt, carry):
        ms, ls, accs, seen = carry
        off = pl.multiple_of(kt * KEY_TILE, KEY_TILE)
        key = key_ref[kt]
        eq = key == thr_w
        before = seen + _dot(jnp.where(eq, 1.0, 0.0).astype(BF16), tri_ref[...])
        sel = ((key > thr_w) | (eq & (before < need_w))) & (off + lane_t <= pos)
        bias = jnp.where(sel, 0.0, NEG_INF)[None]
        new_m, new_l, new_acc = [], [], []
        for g in range(B_KV_HEADS):
            sc = _dot_nt(q8[g], bk_ref[g, pl.ds(off, KEY_TILE), :]).reshape(hg, tq, KEY_TILE) + bias
            m_new = jnp.maximum(ms[g], jnp.max(sc, axis=-1, keepdims=True))
            alpha = jnp.exp(ms[g] - m_new)
            pe = jnp.exp(sc - m_new)
            new_m.append(m_new)
            new_l.append(alpha * ls[g] + jnp.sum(pe, axis=-1, keepdims=True))
            pv = _dot(pe.reshape(r4, KEY_TILE).astype(BF16), bv_ref[g, pl.ds(off, KEY_TILE), :])
            new_acc.append(alpha.reshape(r4, 1) * accs[g] + pv)
        seen = seen + jnp.sum(jnp.where(eq, 1.0, 0.0), axis=-1, keepdims=True)
        return tuple(new_m), tuple(new_l), tuple(new_acc), seen

    init = (tuple(jnp.full((hg, tq, 1), NEG_INF, F32) for _ in range(B_KV_HEADS)),
            tuple(jnp.zeros((hg, tq, 1), F32) for _ in range(B_KV_HEADS)),
            tuple(jnp.zeros((r4, LANES), F32) for _ in range(B_KV_HEADS)), jnp.zeros((tq, 1), F32))
    _, ls, accs, _ = lax.fori_loop(0, nkt, step, init)
    o = _pairs(jnp.concatenate([a / l.reshape(r4, 1) for a, l in zip(accs, ls)], axis=0), tq, 4)
    for c in range(4):
        o_ref[:, c * LANES:(c + 1) * LANES] = o[c]


def _dsa(iq, bq, misc, ika, batt, *, tq, pos0, k_sel):
    B, T, _ = bq.shape
    Lk = ika.shape[1]
    nt = T // tq
    tri = jnp.asarray(np.triu(np.ones((KEY_TILE, KEY_TILE)), 1), BF16)
    in_specs = [
        pl.BlockSpec((None, tq, 256), lambda b, t: (b, t, 0)),
        pl.BlockSpec((None, tq, 512), lambda b, t: (b, t, 0)),
        pl.BlockSpec((None, tq, LANES), lambda b, t: (b, t, 0)),
        _single((None, Lk, LANES), lambda b, t: (b, 0, 0)),
        _single((None, None, B_KV_HEADS, Lk, LANES), lambda b, t: (b, 0, 0, 0, 0)),
        _single((None, None, B_KV_HEADS, Lk, LANES), lambda b, t: (b, 1, 0, 0, 0)),
        _single(tri.shape, lambda b, t: (0, 0)),
    ]
    scratch = [pltpu.VMEM((Lk // KEY_TILE, tq, KEY_TILE), I32)]
    kern = functools.partial(_dsa_kernel, tq=tq, pos0=pos0, k_sel=k_sel)
    return pl.pallas_call(
        kern, grid=(B, nt), in_specs=in_specs,
        out_specs=pl.BlockSpec((None, tq, 512), lambda b, t: (b, t, 0)),
        out_shape=jax.ShapeDtypeStruct((B, T, 512), F32), scratch_shapes=scratch,
        compiler_params=_cparams(("parallel", "arbitrary")), name="dsa",
    )(iq, bq, misc, ika, batt, batt, tri)


GATHER_PAGES = 8


def _gather_kernel(pt_ref, cmpk_hbm, cmpv_hbm, slck_hbm, slcv_hbm, bk_hbm, bv_hbm, idx_hbm,
                   akv_ref, bkv_ref, misc_ref,
                   cmpk_ref, cmpv_ref, a_att_ref, b_att_ref, ika_ref,
                   b0, b1, b2, b3, b4, sem, *, nfull, ns):
    b, c = pl.program_id(0), pl.program_id(1)
    rows = GATHER_PAGES * PAGE_SIZE
    bufs = (b0, b1, b2, b3, b4)
    low = _lane((rows, LANES)) < HEAD_DIM

    def emit(xs):
        for i, (ref, s) in enumerate(((a_att_ref, 0), (a_att_ref, 1), (b_att_ref, 0), (b_att_ref, 1))):
            ref[s, 0] = jnp.where(low, xs[i], 0.0).astype(BF16)
            ref[s, 1] = jnp.where(low, pltpu.roll(xs[i], HEAD_DIM, 1), 0.0).astype(BF16)
        ika_ref[...] = jnp.where(low, xs[4], 0.0).astype(BF16)

    @pl.when(c < nfull)
    def _():
        copies = []
        for p in range(GATHER_PAGES):
            page = pt_ref[b, c * GATHER_PAGES + p]
            dst = pl.ds(p * PAGE_SIZE, PAGE_SIZE)
            copies.append(pltpu.make_async_copy(cmpk_hbm.at[page], cmpk_ref.at[0, dst], sem.at[0]))
            copies.append(pltpu.make_async_copy(cmpv_hbm.at[page], cmpv_ref.at[0, dst], sem.at[1]))
            for i, pool in enumerate((slck_hbm, slcv_hbm, bk_hbm, bv_hbm, idx_hbm)):
                copies.append(pltpu.make_async_copy(pool.at[page], bufs[i].at[dst], sem.at[2 + i]))
        for cp in copies:
            cp.start()
        for cp in copies:
            cp.wait()
        xi = b4[...]
        emit([b0[...], b1[...], b2[...], b3[...], jnp.concatenate([xi, jnp.zeros_like(xi)], axis=1)])

    @pl.when(c == nfull)
    def _():
        cmpk_ref[...] = jnp.zeros(cmpk_ref.shape, F32)
        cmpv_ref[...] = jnp.zeros(cmpv_ref.shape, F32)
        fill = jnp.zeros((rows - ns, LANES), F32)
        new = [akv_ref[2], akv_ref[3], bkv_ref[0], bkv_ref[1], misc_ref[...]]
        emit([jnp.concatenate([x, fill], axis=0) for x in new])


def _gather_pages(pools, page_table, kv, bkv, misc):
    DB, npg = page_table.shape
    ns = kv.shape[2]
    assert npg % GATHER_PAGES == 0
    nfull = npg // GATHER_PAGES
    rows = GATHER_PAGES * PAGE_SIZE
    L = (nfull + 1) * rows
    any_spec = pl.BlockSpec(memory_space=pl.ANY)
    new = lambda n: pl.BlockSpec((None, n, ns, LANES), lambda b, c, pt: (b, 0, 0, 0))
    raw = pl.BlockSpec((None, rows, LANES), lambda b, c, pt: (b, c, 0))
    dma_dst = pl.BlockSpec((1, rows, LANES), lambda b, c, pt: (b, c, 0))
    att = pl.BlockSpec((None, 2, A_GROUPS, rows, LANES), lambda b, c, pt: (b, 0, 0, c, 0))
    grid_spec = pltpu.PrefetchScalarGridSpec(
        num_scalar_prefetch=1, grid=(DB, nfull + 1),
        in_specs=[any_spec] * 7 + [new(6), new(2), pl.BlockSpec((None, ns, LANES), lambda b, c, pt: (b, 0, 0))],
        out_specs=[dma_dst, dma_dst, att, att, raw],
        scratch_shapes=[pltpu.VMEM((rows, LANES), F32)] * 4 + [pltpu.VMEM((rows, pools[6].shape[-1]), F32),
                                                                pltpu.SemaphoreType.DMA((7,))])
    out_shape = [jax.ShapeDtypeStruct((DB, L, LANES), F32), jax.ShapeDtypeStruct((DB, L, LANES), F32),
                 jax.ShapeDtypeStruct((DB, 2, A_GROUPS, L, LANES), BF16),
                 jax.ShapeDtypeStruct((DB, 2, B_KV_HEADS, L, LANES), BF16),
                 jax.ShapeDtypeStruct((DB, L, LANES), BF16)]
    return pl.pallas_call(
        functools.partial(_gather_kernel, nfull=nfull, ns=ns), grid_spec=grid_spec, out_shape=out_shape,
        compiler_params=_cparams(("parallel", "arbitrary")), name="gather_pages",
    )(page_table, *pools, kv, bkv, misc)


def _topk_rows(s, k, payload=None):
    n = s.shape[0]
    rows = _row(s.shape)
    vals, idxs = [], []
    for _ in range(k):
        m = jnp.max(s, axis=0, keepdims=True)
        idx = jnp.min(jnp.where(s == m, rows, n), axis=0, keepdims=True)
        hit = rows == idx
        vals.append(m)
        idxs.append(idx if payload is None else jnp.max(jnp.where(hit, payload, -1), axis=0, keepdims=True))
        s = jnp.where(hit, -jnp.inf, s)
    return jnp.concatenate(vals, axis=0), jnp.concatenate(idxs, axis=0)


def _route_kernel(x_ref, oa_ref, ob_ref, gates_ref, woa_ref, wob_ref, wo_ref, g2_ref, wq_ref, sub1_ref, sub2_ref,
                  h_ref, hn_ref, eid_ref, gate_ref):
    gates = gates_ref[...]
    ya = _dot(oa_ref[...].astype(BF16), woa_ref[...])
    yb = _dot(ob_ref[...].astype(BF16), wob_ref[...])
    mix = gates[:, :D_MODEL] * ya + gates[:, D_MODEL:] * yb
    h = x_ref[...] + _dot(mix.astype(BF16), wo_ref[...])
    h_ref[...] = h
    hn = h * lax.rsqrt(jnp.mean(h * h, axis=-1, keepdims=True) + EPS) * g2_ref[...]
    hn_ref[...] = hn
    q = _dot(hn.astype(BF16), wq_ref[...])
    half = PEER_KEY_DIM // 2
    for hd in range(PEER_HEADS):
        q1 = q[:, hd * PEER_KEY_DIM:hd * PEER_KEY_DIM + half].astype(BF16)
        q2 = q[:, hd * PEER_KEY_DIM + half:(hd + 1) * PEER_KEY_DIM].astype(BF16)
        v1, i1 = _topk_rows(_dot_nt(sub1_ref[hd], q1), PEER_TOPK)
        v2, i2 = _topk_rows(_dot_nt(sub2_ref[hd], q2), PEER_TOPK)
        cand = jnp.concatenate([v1[r:r + 1] + v2 for r in range(PEER_TOPK)], axis=0)
        cid = jnp.concatenate([i1[r:r + 1] * PEER_NKEYS + i2 for r in range(PEER_TOPK)], axis=0)
        vt, et = _topk_rows(cand, PEER_TOPK, payload=cid)
        e = jnp.exp(vt - vt[0:1])
        eid_ref[0, hd * PEER_TOPK:(hd + 1) * PEER_TOPK, :] = et
        gate_ref[0, hd * PEER_TOPK:(hd + 1) * PEER_TOPK, :] = e / jnp.sum(e, axis=0, keepdims=True)


def _route(x, oa, ob, gates, woa, wob, wo, g2, wq, sub1, sub2, tm):
    N, D = x.shape
    nt = N // tm
    tok = lambda c: pl.BlockSpec((tm, c), lambda i: (i, 0))
    const = lambda a: pl.BlockSpec(a.shape, lambda i: (0,) * a.ndim)
    slot = pl.BlockSpec((1, PEER_HEADS * PEER_TOPK, tm), lambda i: (i, 0, 0))
    consts = (woa, wob, wo, g2, wq, sub1, sub2)
    return pl.pallas_call(
        _route_kernel, grid=(nt,),
        in_specs=[tok(D), tok(512), tok(512), tok(2 * D)] + [const(a) for a in consts],
        out_specs=(tok(D), tok(D), slot, slot),
        out_shape=(jax.ShapeDtypeStruct((N, D), F32), jax.ShapeDtypeStruct((N, D), F32),
                   jax.ShapeDtypeStruct((nt, PEER_HEADS * PEER_TOPK, tm), I32),
                   jax.ShapeDtypeStruct((nt, PEER_HEADS * PEER_TOPK, tm), F32)),
        compiler_params=_cparams(("parallel",)), name="route",
    )(x, oa, ob, gates, *consts)


def _pack_table(t):
    E, D = t.shape
    b = lax.bitcast_convert_type(t.astype(BF16), jnp.uint16).astype(jnp.uint32)
    w = b[:, :D // 2] | (b[:, D // 2:] << 16)
    return lax.bitcast_convert_type(w, I32).reshape(E, D // 256, LANES)


def _unpack_row(w):
    return pltpu.bitcast(w << 16, F32), pltpu.bitcast(w & jnp.int32(-65536), F32)


def _load_table(tab_hbm, tab_vmem, sem):
    @pl.when(pl.program_id(0) == 0)
    def _():
        cp = pltpu.make_async_copy(tab_hbm, tab_vmem, sem)
        cp.start()
        cp.wait()


PEER_LANE_GROUP = 8
PEER_USES_PER_ROW = LANES // PEER_LANE_GROUP


def _peer_act_kernel(eid_ref, x_ref, gsum_ref, tab_hbm, act_ref, tab_vmem, sem, stage0_ref, stage1_ref, v_ref,
                     *, tt, n_slot):
    _load_table(tab_hbm, tab_vmem, sem)
    rows_per_use = stage0_ref.shape[0] // n_slot
    lane, sub = _lane((PEER_LANE_GROUP, LANES)), _row((PEER_LANE_GROUP, LANES))
    masks = [((lane >> 3) == c) & ((lane & 7) == sub) for c in range(PEER_USES_PER_ROW)]
    groups = n_slot // PEER_USES_PER_ROW
    vrows = groups * PEER_LANE_GROUP

    def one(t, stage_ref):
        base = t * n_slot
        for j in range(n_slot):
            stage_ref[rows_per_use * j:rows_per_use * (j + 1), :] = tab_vmem[eid_ref[base + j]]
        bm = pltpu.bitcast(stage_ref[...], BF16)
        xrep = jnp.concatenate([x_ref[t]] * PEER_USES_PER_ROW, axis=0).astype(BF16)
        g = _dot_nt(bm, xrep)
        vs = []
        for m in range(groups):
            v = jnp.zeros((PEER_LANE_GROUP, LANES), F32)
            for c in range(PEER_USES_PER_ROW):
                r0 = PEER_LANE_GROUP * (PEER_USES_PER_ROW * m + c)
                v = v + jnp.where(masks[c], g[r0:r0 + PEER_LANE_GROUP], 0.0)
            vs.append(v)
        v_ref[pl.ds(pl.multiple_of(t * vrows, vrows), vrows), :] = jnp.concatenate(vs, axis=0)

    def pair(i, _):
        one(2 * i, stage0_ref)
        one(2 * i + 1, stage1_ref)
        return 0

    lax.fori_loop(0, tt // 2, pair, 0)
    a = _split_dot(v_ref[...], gsum_ref[...])
    act_ref[...] = jnp.sum(a.reshape(tt * groups, PEER_LANE_GROUP, LANES), axis=1).reshape(tt, groups, LANES)


def _gelu_gate_kernel(act_ref, gate_ref, w_ref):
    act = act_ref[...]
    w_ref[...] = gate_ref[...] * (act * (lax.erf(act * (2.0 ** -0.5)) + 1.0) * 0.5)


def _peer_out_kernel(eid_ref, w_ref, h_ref, tab_hbm, y_ref, tab_vmem, sem, *, tt, n_slot):
    _load_table(tab_hbm, tab_vmem, sem)
    half = h_ref.shape[1] // 2
    chunk = 16

    def token(t, _):
        def part(c, acc):
            acc = list(acc)
            base = t * n_slot + c * chunk
            for j in range(chunk):
                lo, hi = _unpack_row(tab_vmem[eid_ref[base + j]])
                wj = w_ref[base + j]
                acc[2 * (j % 2)] = acc[2 * (j % 2)] + wj * lo
                acc[2 * (j % 2) + 1] = acc[2 * (j % 2) + 1] + wj * hi
            return tuple(acc)

        acc = lax.fori_loop(0, n_slot // chunk, part, tuple(jnp.zeros((half, LANES), F32) for _ in range(4)))
        y_ref[t] = h_ref[t] + jnp.concatenate([acc[0] + acc[2], acc[1] + acc[3]], axis=0)
        return 0

    lax.fori_loop(0, tt, token, 0)


def _peer(eid, gate, hn, h, u_tab, v_tab, tt):
    nt, n_slot, _ = eid.shape
    N, D = h.shape
    sub = D // LANES
    assert n_slot == LANES and sub == PEER_LANE_GROUP
    eid_flat = eid.transpose(0, 2, 1).reshape(-1)
    gate_t = gate.transpose(0, 2, 1).reshape(N, n_slot)
    xr = hn.reshape(N, 2, sub // 2, LANES).transpose(0, 2, 1, 3).reshape(N, sub, LANES)
    gsum = jnp.asarray((np.arange(LANES)[:, None] // PEER_LANE_GROUP == np.arange(LANES)[None, :]), BF16)
    smem = lambda: pl.BlockSpec((n_slot * tt,), lambda i: (i,), memory_space=pltpu.SMEM)
    rows = pl.BlockSpec((tt, sub, LANES), lambda i: (i, 0, 0))
    any_spec = pl.BlockSpec(memory_space=pl.ANY)
    tab_scratch = [pltpu.VMEM(u_tab.shape, I32), pltpu.SemaphoreType.DMA(())]
    act = pl.pallas_call(
        functools.partial(_peer_act_kernel, tt=tt, n_slot=n_slot), grid=(nt,),
        in_specs=[smem(), rows, pl.BlockSpec((LANES, LANES), lambda i: (0, 0)), any_spec], out_specs=rows,
        out_shape=jax.ShapeDtypeStruct((N, sub, LANES), F32),
        scratch_shapes=tab_scratch + [pltpu.VMEM((n_slot * u_tab.shape[1], LANES), I32)] * 2
        + [pltpu.VMEM((tt * n_slot // PEER_USES_PER_ROW * PEER_LANE_GROUP, LANES), F32)],
        compiler_params=_cparams(("arbitrary",)), name="peer_act",
    )(eid_flat, xr, gsum, u_tab)
    act_t = act[:, :, :PEER_USES_PER_ROW].reshape(N, n_slot)
    tw = 1024 if N % 1024 == 0 else N
    tile = pl.BlockSpec((tw, n_slot), lambda i: (i, 0))
    w = pl.pallas_call(
        _gelu_gate_kernel, grid=(N // tw,), in_specs=[tile, tile], out_specs=tile,
        out_shape=jax.ShapeDtypeStruct((N, n_slot), F32),
        compiler_params=_cparams(("parallel",)), name="gelu_gate",
    )(act_t, gate_t)
    y = pl.pallas_call(
        functools.partial(_peer_out_kernel, tt=tt, n_slot=n_slot), grid=(nt,),
        in_specs=[smem(), smem(), rows, any_spec], out_specs=rows,
        out_shape=jax.ShapeDtypeStruct((N, sub, LANES), F32), scratch_shapes=tab_scratch,
        compiler_params=_cparams(("arbitrary",)), name="peer_out",
    )(eid_flat, w.reshape(-1), h.reshape(N, sub, LANES), v_tab)
    return y.reshape(N, D)


def _relayout_w_in(w_in):
    names = (('a_q', 512), ('a_k_cmp', 128), ('a_v_cmp', 128), ('a_k_slc', 128), ('a_v_slc', 128),
             ('a_k_win', 128), ('a_v_win', 128), ('a_gate', 24), ('b_q', 512), ('b_k', 128), ('b_v', 128),
             ('i_q', 256), ('i_w', 4), ('i_k', 64), ('merge', 2 * D_MODEL))
    sec, o = {}, 0
    for name, n in names:
        sec[name] = w_in[:, o:o + n]
        o += n
    pad = jnp.zeros((w_in.shape[0], LANES - 64 - 4 - 24), w_in.dtype)
    order = ['a_q', 'a_k_cmp', 'a_v_cmp', 'a_k_slc', 'a_v_slc', 'a_k_win', 'a_v_win', 'b_q', 'b_k', 'b_v',
             'i_q', 'i_k', 'i_w', 'a_gate']
    return jnp.concatenate([sec[n] for n in order] + [pad, sec['merge']], axis=1).astype(BF16)


def _group_lanes(x):
    z = jnp.zeros_like(x[..., :HEAD_DIM])
    parts = [jnp.concatenate([x[..., g * HEAD_DIM:(g + 1) * HEAD_DIM], z], axis=-1) for g in range(2)]
    return jnp.stack(parts, axis=-3).astype(BF16)


def _pad_rows(x, total, axis):
    pad = [(0, 0)] * x.ndim
    pad[axis] = (0, total - x.shape[axis])
    return jnp.pad(x, pad)


def kernel(x_prompt, x_sample, cache_a_cmp_k, cache_a_cmp_v, cache_a_slc_k, cache_a_slc_v, state_a_win_k, state_a_win_v, cache_b_k, cache_b_v, cache_b_idx_k, page_table, norm1_g, w_in, g_qa, g_ka_cmp, g_ka_slc, g_ka_win, g_qb, g_kb, g_kidx, w_cmp_k, pe_cmp_k, w_cmp_v, pe_cmp_v, g_kc, w_out_a, w_out_b, w_o, norm2_g, w_peer_q, peer_subkeys, peer_u, peer_v):
    B, S, D = x_prompt.shape
    DB, NS, _ = x_sample.shape
    npg = page_table.shape[1]
    past = npg * PAGE_SIZE
    assert D == D_MODEL and S % KEY_TILE == 0 and past % KEY_TILE == 0 and NS < CMP_STRIDE
    assert state_a_win_k.shape[1] == WINDOW

    wp = _relayout_w_in(w_in)
    ones = jnp.ones((HEAD_DIM,), F32)
    gains = jnp.stack([jnp.tile(g, 2) for g in (g_qa, g_ka_cmp, g_ka_slc, g_ka_win, g_qb, g_kb)]
                      + [jnp.concatenate([g_kidx, ones]), jnp.tile(ones, 2)])
    wk2, pek2 = _compress_weights(w_cmp_k, pe_cmp_k)
    wv2, pev2 = _compress_weights(w_cmp_v, pe_cmp_v)
    woa, wob, wo = w_out_a.astype(BF16), w_out_b.astype(BF16), w_o.astype(BF16)
    wq = w_peer_q.reshape(D, PEER_HEADS * PEER_KEY_DIM).astype(BF16)
    sub1, sub2 = peer_subkeys[0].astype(BF16), peer_subkeys[1].astype(BF16)
    g2 = norm2_g.reshape(1, D)
    u_tab, v_tab = _pack_table(peer_u), _pack_table(peer_v)

    def finish(x, oa, ob, gates):
        n = x.shape[0] * x.shape[1]
        tt = 128 if n % 128 == 0 else n
        h, hn, eid, gate = _route(x.reshape(n, D), oa.reshape(n, 512), ob.reshape(n, 512),
                                  gates.reshape(n, 2 * D), woa, wob, wo, g2, wq, sub1, sub2, tt)
        return _peer(eid, gate, hn, h, u_tab, v_tab, tt).reshape(x.shape)

    def heads(a, n):
        return a.reshape(a.shape[0], a.shape[1], n, HEAD_DIM)

    total = past + NS
    aq, bq, iq, kv, bkv, misc, ika, gates, a_att, b_att = _project(
        x_sample, past + jnp.arange(NS), norm1_g, wp, gains, NS)
    flat = lambda c: c.reshape(c.shape[0], PAGE_SIZE, -1)
    cmpk_s, cmpv_s, a_att_s, b_att_s, ika_s = _gather_pages(
        [flat(c) for c in (cache_a_cmp_k, cache_a_cmp_v, cache_a_slc_k, cache_a_slc_v, cache_b_k, cache_b_v,
                           cache_b_idx_k)], page_table, kv, bkv, misc)
    nch_s = total // CMP_STRIDE
    assert nch_s * CMP_STRIDE <= past
    kc, vc = _compress((cmpk_s, 0), (cmpv_s, 0), nch_s, wk2, wv2, pek2, pev2, g_kc)
    wl = -(-(WINDOW + NS) // LANES) * LANES
    kw_s = _pad_rows(jnp.concatenate([_group_lanes(state_a_win_k.reshape(DB, WINDOW, LANES)), a_att[:, 2]], axis=2), wl, 2)
    vw_s = _pad_rows(jnp.concatenate([_group_lanes(state_a_win_v.reshape(DB, WINDOW, LANES)), a_att[:, 3]], axis=2), wl, 2)
    o_a = _nsa(aq, misc, kc, vc, a_att_s, kw_s, vw_s, tq=NS, pos0=past, n_cmp=nch_s - 1,
               n_slc=-(-total // SLC_BLOCK))
    o_b = _dsa(iq, bq, misc, ika_s, b_att_s, tq=NS, pos0=past, k_sel=min(DSA_TOPK_MAX, total // 4))
    y_sample = finish(x_sample, o_a, o_b, gates)
    s_out = [heads(kv[:, i], A_GROUPS) for i in range(4)]
    s_out += [jnp.concatenate([state_a_win_k[:, NS:], heads(kv[:, 4], A_GROUPS)], axis=1),
              jnp.concatenate([state_a_win_v[:, NS:], heads(kv[:, 5], A_GROUPS)], axis=1)]
    s_out += [heads(bkv[:, 0], B_KV_HEADS), heads(bkv[:, 1], B_KV_HEADS), misc[:, :, :IDX_DIM]]

    tq = 128
    aq, bq, iq, kv, bkv, misc, ika, gates, a_att, b_att = _project(
        x_prompt, jnp.arange(S), norm1_g, wp, gains, 256)
    nch = S // CMP_STRIDE
    kc, vc = _compress((kv, 0), (kv, 1), nch, wk2, wv2, pek2, pev2, g_kc)
    front = ((0, 0), (0, 0), (WINDOW, 0), (0, 0))
    o_a = _nsa(aq, misc, kc, vc, a_att, jnp.pad(a_att[:, 2], front), jnp.pad(a_att[:, 3], front),
               tq=tq, pos0=0, n_cmp=nch - 1, n_slc=-(-S // SLC_BLOCK))
    o_b = _dsa(iq, bq, misc, ika, b_att, tq=tq, pos0=0, k_sel=min(DSA_TOPK_MAX, S // 4))
    y_prompt = finish(x_prompt, o_a, o_b, gates)
    win_p = min(WINDOW, S)
    p_out = [heads(kv[:, i], A_GROUPS) for i in range(4)]
    p_out += [heads(kv[:, i, S - win_p:], A_GROUPS) for i in (4, 5)]
    p_out += [heads(bkv[:, 0], B_KV_HEADS), heads(bkv[:, 1], B_KV_HEADS), misc[:, :, :IDX_DIM]]
    return (y_prompt, y_sample, *p_out, *s_out)
```

```python
import functools

import numpy as np
import jax
import jax.numpy as jnp
from jax import lax
from jax.experimental import pallas as pl
from jax.experimental.pallas import tpu as pltpu

F32 = jnp.float32
BF16 = jnp.bfloat16
I32 = jnp.int32

LANES = 128
HEAD_DIM = 64
ROPE_DIM = HEAD_DIM // 4
ROPE_THETA = 500000.0
QK_SCALE = HEAD_DIM ** -0.5
EPS = 1e-6
NEG_INF = -1e30
A_HEADS = 8
A_GROUPS = 2
CMP_STRIDE = 16
CMP_LEN = 32
SLC_BLOCK = 64
SLC_TOP = 16
FORCE_BONUS = 1e4
WINDOW = 512
B_HEADS = 8
B_KV_HEADS = 2
IDX_HEADS = 4
IDX_DIM = 64
IDX_SCALE = (IDX_HEADS * IDX_DIM) ** -0.5
DSA_TOPK_MAX = 256
PEER_HEADS = 8
PEER_NKEYS = 128
PEER_KEY_DIM = 256
PEER_TOPK = 16
PAGE_SIZE = 128
D_MODEL = 1024

KEY_TILE = 512
VMEM_LIMIT = 56 * 1024 * 1024

OFF_AQ, OFF_KV, OFF_BQ, OFF_BKV, OFF_IQ, OFF_MISC, OFF_MERGE = 0, 512, 1280, 1792, 2048, 2304, 2432
D_PROJ = OFF_MERGE + 2 * D_MODEL
MISC_IW = 64
MISC_GATE = 68

_NT = (((1,), (1,)), ((), ()))


def _dot(a, b):
    return jnp.dot(a, b, preferred_element_type=F32)


def _dot_nt(a, b):
    return lax.dot_general(a, b, _NT, preferred_element_type=F32)


def _split_dot(x, w):
    hi = x.astype(BF16)
    lo = (x - hi.astype(F32)).astype(BF16)
    return _dot(hi, w) + _dot(lo, w)


def _lane(shape):
    return lax.broadcasted_iota(I32, shape, len(shape) - 1)


def _row(shape):
    return lax.broadcasted_iota(I32, shape, len(shape) - 2)


def _order_key(x):
    b = pltpu.bitcast(x + 0.0, I32)
    return b ^ ((b >> 31) & jnp.int32(0x7FFFFFFF))


def _cparams(sem):
    return pltpu.CompilerParams(dimension_semantics=sem, vmem_limit_bytes=VMEM_LIMIT)


def _single(shape, index_map):
    return pl.BlockSpec(shape, index_map, pipeline_mode=pl.Buffered(1))


def _proj_kernel(x_ref, g1_ref, w_ref, gains_ref, bd_ref, rc_ref, rs1_ref, rs2_ref,
                 aq_ref, bq_ref, iq_ref, kv_ref, bkv_ref, misc_ref, ika_ref, gates_ref,
                 akv_att_ref, bkv_att_ref):
    x = x_ref[0]
    xn = x * lax.rsqrt(jnp.mean(x * x, axis=-1, keepdims=True) + EPS) * g1_ref[...]
    xb = xn.astype(BF16)
    bd = bd_ref[...]
    rc, rs1, rs2 = rc_ref[...], rs1_ref[...], rs2_ref[...]
    lane = _lane((x.shape[0], LANES))
    low = lane < HEAD_DIM

    def proj(a, b):
        return _dot(xb, w_ref[:, a:b])

    def norm(y, g):
        return y * lax.rsqrt(_split_dot(y * y, bd) * (1.0 / HEAD_DIM) + EPS) * g

    def rope(y):
        return y * rc + pltpu.roll(y, LANES - ROPE_DIM // 2, 1) * rs1 + pltpu.roll(y, ROPE_DIM // 2, 1) * rs2

    def att_copy(ref, s, y):
        ref[0, s, 0] = jnp.where(low, y, 0.0).astype(BF16)
        ref[0, s, 1] = jnp.where(low, pltpu.roll(y, HEAD_DIM, 1), 0.0).astype(BF16)

    z = proj(OFF_AQ, OFF_AQ + 512)
    for c in range(4):
        aq_ref[0, :, c * LANES:(c + 1) * LANES] = rope(norm(z[:, c * LANES:(c + 1) * LANES], gains_ref[0:1])) * QK_SCALE
    z = proj(OFF_KV, OFF_KV + 768)
    for s in range(6):
        y = z[:, s * LANES:(s + 1) * LANES]
        if s % 2 == 0:
            y = rope(norm(y, gains_ref[1 + s // 2:2 + s // 2]))
        kv_ref[0, s] = y
        if s >= 2:
            att_copy(akv_att_ref, s - 2, y)
    z = proj(OFF_BQ, OFF_BQ + 512)
    for c in range(4):
        bq_ref[0, :, c * LANES:(c + 1) * LANES] = rope(norm(z[:, c * LANES:(c + 1) * LANES], gains_ref[4:5])) * QK_SCALE
    z = proj(OFF_BKV, OFF_BKV + 256)
    yk = rope(norm(z[:, :LANES], gains_ref[5:6]))
    yv = z[:, LANES:]
    bkv_ref[0, 0] = yk
    bkv_ref[0, 1] = yv
    att_copy(bkv_att_ref, 0, yk)
    att_copy(bkv_att_ref, 1, yv)
    z = proj(OFF_IQ, OFF_IQ + 256)
    for c in range(2):
        iq_ref[0, :, c * LANES:(c + 1) * LANES] = rope(z[:, c * LANES:(c + 1) * LANES])
    z = proj(OFF_MISC, OFF_MISC + LANES)
    ik = rope(norm(z, gains_ref[6:7]))
    misc_ref[0] = jnp.where(low, ik, jnp.where(lane < MISC_GATE, z * IDX_SCALE, jax.nn.sigmoid(z)))
    ika_ref[0] = jnp.where(low, ik, 0.0).astype(BF16)
    gates_ref[0] = jax.nn.sigmoid(proj(OFF_MERGE, OFF_MERGE + 2 * D_MODEL))


def _project(x, pos, norm1_g, wp, gains, tm):
    B, T, D = x.shape
    half = ROPE_DIM // 2
    inv = jnp.power(ROPE_THETA, -jnp.arange(half, dtype=F32) / half)
    ang = pos.astype(F32)[:, None] * inv[None, :]
    cos, sin = jnp.cos(ang), jnp.sin(ang)
    one = jnp.ones((T, HEAD_DIM - ROPE_DIM), F32)
    zero = jnp.zeros((T, HEAD_DIM - ROPE_DIM), F32)
    zh = jnp.zeros((T, half), F32)
    rc = jnp.tile(jnp.concatenate([cos, cos, one], axis=1), (1, 2))
    rs1 = jnp.tile(jnp.concatenate([-sin, zh, zero], axis=1), (1, 2))
    rs2 = jnp.tile(jnp.concatenate([zh, sin, zero], axis=1), (1, 2))
    bd = jnp.asarray(np.kron(np.eye(2), np.ones((HEAD_DIM, HEAD_DIM))), BF16)
    nt = T // tm
    tok = lambda c: pl.BlockSpec((1, tm, c), lambda b, t: (b, t, 0))
    const = lambda shape: pl.BlockSpec(shape, lambda b, t: (0,) * len(shape))
    tab = pl.BlockSpec((tm, LANES), lambda b, t: (t, 0))
    stack = lambda n: pl.BlockSpec((1, n, tm, LANES), lambda b, t: (b, 0, t, 0))
    att = lambda n: pl.BlockSpec((1, n, A_GROUPS, tm, LANES), lambda b, t: (b, 0, 0, t, 0))
    out_shape = (
        jax.ShapeDtypeStruct((B, T, 512), F32),
        jax.ShapeDtypeStruct((B, T, 512), F32),
        jax.ShapeDtypeStruct((B, T, 256), F32),
        jax.ShapeDtypeStruct((B, 6, T, LANES), F32),
        jax.ShapeDtypeStruct((B, 2, T, LANES), F32),
        jax.ShapeDtypeStruct((B, T, LANES), F32),
        jax.ShapeDtypeStruct((B, T, LANES), BF16),
        jax.ShapeDtypeStruct((B, T, 2 * D_MODEL), F32),
        jax.ShapeDtypeStruct((B, 4, A_GROUPS, T, LANES), BF16),
        jax.ShapeDtypeStruct((B, 2, B_KV_HEADS, T, LANES), BF16),
    )
    out_specs = (tok(512), tok(512), tok(256), stack(6), stack(2), tok(LANES), tok(LANES),
                 tok(2 * D_MODEL), att(4), att(2))
    return pl.pallas_call(
        _proj_kernel, grid=(B, nt), out_shape=out_shape, out_specs=out_specs,
        in_specs=[tok(D), const((1, D)), const((D, D_PROJ)), const((8, LANES)), const((LANES, LANES)),
                  tab, tab, tab],
        compiler_params=_cparams(("parallel", "parallel")), name="proj",
    )(x, norm1_g.reshape(1, D), wp, gains, bd, rc, rs1, rs2)


def _compress_kernel(k_ref, v_ref, wk_ref, wv_ref, pek_ref, pev_ref, gkc_ref, kc_ref, vc_ref, *, nch):
    def one(src_ref, w_ref, pe_ref, out_ref, gain):
        r = None
        for j in range(CMP_STRIDE):
            d = _dot(src_ref[pl.ds(j, nch, stride=CMP_STRIDE), :].astype(BF16), w_ref[j * LANES:(j + 1) * LANES, :])
            r = d if r is None else r + d
        rpe = _dot(pe_ref[...].astype(BF16), w_ref[...])
        for g in range(A_GROUPS):
            lo = r[:, g * LANES:(g + 1) * LANES]
            hi = r[:, (2 + g) * LANES:(3 + g) * LANES]
            c = rpe[0:1, g * LANES:(g + 1) * LANES] + rpe[1:2, (2 + g) * LANES:(3 + g) * LANES]
            y = lo + pltpu.roll(hi, nch - 1, 0) + c
            if gain is not None:
                ms = jnp.sum(y * y, axis=-1, keepdims=True) * (1.0 / HEAD_DIM)
                y = y * lax.rsqrt(ms + EPS) * gain
            out_ref[0, g] = y.astype(BF16)

    one(k_ref, wk_ref, pek_ref, kc_ref, gkc_ref[...])
    one(v_ref, wv_ref, pev_ref, vc_ref, None)


def _compress_weights(w_c, pe):
    half = CMP_STRIDE
    w2 = jnp.zeros((half, A_GROUPS, HEAD_DIM, 4, LANES), F32)
    for g in range(A_GROUPS):
        w2 = w2.at[:, g, :, g, :HEAD_DIM].set(w_c[:half])
        w2 = w2.at[:, g, :, 2 + g, :HEAD_DIM].set(w_c[half:])
    w2 = w2.reshape(half * A_GROUPS * HEAD_DIM, 4 * LANES).astype(BF16)
    pe_lo = jnp.tile(pe[:half, None, :], (1, A_GROUPS, 1)).reshape(1, -1)
    pe_hi = jnp.tile(pe[half:, None, :], (1, A_GROUPS, 1)).reshape(1, -1)
    pe2 = jnp.concatenate([pe_lo, pe_hi, jnp.zeros((6, pe_lo.shape[1]), F32)], axis=0)
    return w2, pe2


def _compress(k_src, v_src, nch, wk2, wv2, pek2, pev2, g_kc):
    B = k_src[0].shape[0]
    gkc = jnp.concatenate([g_kc, jnp.zeros((LANES - HEAD_DIM,), F32)]).reshape(1, LANES)

    def src(a, i):
        if a.ndim == 4:
            return pl.BlockSpec((None, None, nch * CMP_STRIDE, LANES), lambda b: (b, i, 0, 0))
        return pl.BlockSpec((None, nch * CMP_STRIDE, LANES), lambda b: (b, 0, 0))
    const = lambda shape: pl.BlockSpec(shape, lambda b: (0,) * len(shape))
    out = pl.BlockSpec((1, A_GROUPS, nch, LANES), lambda b: (b, 0, 0, 0))
    shp = jax.ShapeDtypeStruct((B, A_GROUPS, nch, LANES), BF16)
    return pl.pallas_call(
        functools.partial(_compress_kernel, nch=nch), grid=(B,), out_shape=(shp, shp), out_specs=(out, out),
        in_specs=[src(*k_src), src(*v_src), const(wk2.shape), const(wv2.shape), const(pek2.shape),
                  const(pev2.shape), const((1, LANES))],
        compiler_params=_cparams(("parallel",)), name="compress",
    )(k_src[0], v_src[0], wk2, wv2, pek2, pev2, gkc)


def _heads_rows(q, n_chunks):
    rows = []
    for c in range(n_chunks):
        ch = q[:, c * LANES:(c + 1) * LANES]
        rows.append(ch.astype(BF16))
        rows.append(pltpu.roll(ch, HEAD_DIM, 1).astype(BF16))
    return jnp.concatenate(rows, axis=0)


def _pairs(o, t, n_chunks):
    return [o[(2 * c) * t:(2 * c + 1) * t] + pltpu.roll(o[(2 * c + 1) * t:(2 * c + 2) * t], HEAD_DIM, 1)
            for c in range(n_chunks)]


def _kth_largest(count_ge, k, shape, bits=1):
    def body(i, t):
        step = lax.shift_left(jnp.int32(1), jnp.int32(32 - bits) - bits * i)
        counts = count_ge([t + step * r for r in range(1, 1 << bits)])
        inc = jnp.zeros(shape, I32)
        for c in counts:
            inc = inc + jnp.where(c >= k, 1, 0)
        return t + inc * step
    return lax.fori_loop(0, 32 // bits, body, jnp.full(shape, jnp.iinfo(jnp.int32).min, I32))


def _softmax_rows(s):
    m = jnp.max(s, axis=-1, keepdims=True)
    e = jnp.exp(s - m)
    return e / jnp.sum(e, axis=-1, keepdims=True)


def _nsa_kernel(q_ref, misc_ref, kc_ref, vc_ref, ks_ref, vs_ref, kw_ref, vw_ref, m_ref, tri_ref, eg_ref,
                o_ref, selb_ref, *, tq, nt, pos0, n_cmp, n_slc):
    nch = kc_ref.shape[0]
    nsp = m_ref.shape[1]
    ncs = nsp // LANES
    t0 = pos0 + pl.program_id(2) * tq
    pos = t0 + _row((tq, 1))
    pos4 = jnp.concatenate([pos] * 4, axis=0)
    q4 = _heads_rows(q_ref[...], 2)
    ones = jnp.ones((LANES, LANES), BF16)

    s = _dot_nt(q4, kc_ref[...])
    blk_c = _lane((4 * tq, nch))
    s = jnp.where((blk_c * CMP_STRIDE + (CMP_LEN - 1) <= pos4) & (blk_c < n_cmp), s, NEG_INF)
    p = jnp.where(pos4 >= CMP_LEN - 1, _softmax_rows(s), 0.0)
    o_cmp = _dot(p.astype(BF16), vc_ref[...])
    psum = p[0:tq] + p[tq:2 * tq] + p[2 * tq:3 * tq] + p[3 * tq:4 * tq]
    imp = _split_dot(psum, m_ref[...])

    blk = _lane((tq, nsp))
    cur = pos >> 6
    forced = (blk == 0) | (blk == cur) | (blk == cur - 1)
    adm = (blk * SLC_BLOCK <= pos) & (blk < n_slc)
    key = _order_key(jnp.where(adm, imp + jnp.where(forced, FORCE_BONUS, 0.0), NEG_INF))
    keys = [key[:, c * LANES:(c + 1) * LANES] for c in range(ncs)]

    def count(pred):
        return sum(_dot(jnp.where(pred(kc), 1.0, 0.0).astype(BF16), ones) for kc in keys)

    thr = _kth_largest(lambda cands: [count(lambda kc: kc >= cand) for cand in cands], float(SLC_TOP),
                       (tq, LANES), bits=4)
    need = float(SLC_TOP) - count(lambda kc: kc > thr)
    thr_w = jnp.concatenate([thr] * ncs, axis=1)
    need_w = jnp.concatenate([need] * ncs, axis=1)
    eq = key == thr_w
    before = _dot(jnp.where(eq, 1.0, 0.0).astype(BF16), tri_ref[...])
    sel = ((key > thr_w) | (eq & (before < need_w))) & adm
    selb = jnp.where(sel, 0.0, NEG_INF)
    for c in range(ncs):
        selb_ref[c] = selb[:, c * LANES:(c + 1) * LANES]

    klast = t0 // KEY_TILE
    blocks_per_tile = KEY_TILE // SLC_BLOCK
    tiles_per_chunk = LANES // blocks_per_tile

    def augment(chunk):
        sb = selb_ref[chunk].astype(BF16)
        return jnp.concatenate([q4, jnp.concatenate([sb] * 4, axis=0)], axis=1)

    qa_all = augment(0) if ncs == 1 else None

    def step(kt, carry, causal):
        m, l, acc = carry
        qa = qa_all if ncs == 1 else augment(kt // tiles_per_chunk)
        off = pl.multiple_of(kt * KEY_TILE, KEY_TILE)
        kblk = (kt % tiles_per_chunk) * blocks_per_tile + (_row((KEY_TILE, LANES)) >> 6)
        et = jnp.where(kblk == _lane((KEY_TILE, LANES)), 1.0, 0.0).astype(BF16)
        ka = jnp.concatenate([ks_ref[pl.ds(off, KEY_TILE), :], et], axis=1)
        sc = _dot_nt(qa, ka)
        if causal:
            sc = jnp.where(off + _lane(sc.shape) <= pos4, sc, NEG_INF)
        m_new = jnp.maximum(m, jnp.max(sc, axis=-1, keepdims=True))
        alpha = jnp.exp(m - m_new)
        pe = jnp.exp(sc - m_new)
        l = alpha * l + jnp.sum(pe, axis=-1, keepdims=True)
        acc = alpha * acc + _dot(pe.astype(BF16), vs_ref[pl.ds(off, KEY_TILE), :])
        return m_new, l, acc

    init = (jnp.full((4 * tq, 1), NEG_INF, F32), jnp.zeros((4 * tq, 1), F32), jnp.zeros((4 * tq, LANES), F32))
    carry = lax.fori_loop(0, klast, lambda kt, c: step(kt, c, False), init)
    _, l, acc = step(klast, carry, True)
    o_slc = acc / l

    wl = kw_ref.shape[0] if nt == 1 else WINDOW + tq
    if nt == 1:
        kw, vw = kw_ref[...], vw_ref[...]
    else:
        ws = pl.multiple_of(pl.program_id(2) * tq, tq)
        kw, vw = kw_ref[pl.ds(ws, wl), :], vw_ref[pl.ds(ws, wl), :]
    sw = _dot_nt(q4, kw)
    pos_w = t0 - WINDOW + _lane(sw.shape)
    rel = pos4 - pos_w
    sw = jnp.where((rel >= 0) & (rel <= WINDOW) & (pos_w >= 0), sw, NEG_INF)
    o_win = _dot(_softmax_rows(sw).astype(BF16), vw)

    gates = _split_dot(misc_ref[...], eg_ref[...])
    pc, ps, pw = _pairs(o_cmp, tq, 2), _pairs(o_slc, tq, 2), _pairs(o_win, tq, 2)
    for c in range(2):
        g = lambda br: gates[:, (2 * br + c) * LANES:(2 * br + c + 1) * LANES]
        o_ref[:, c * LANES:(c + 1) * LANES] = g(0) * pc[c] + g(1) * ps[c] + g(2) * pw[c]


def _cmp_to_slc(nch, nsp, n_cmp, n_slc):
    i = np.arange(nch)[:, None]
    j = np.arange(nsp)[None, :]
    start = i * CMP_STRIDE
    ov = np.minimum(start + CMP_LEN, (j + 1) * SLC_BLOCK) - np.maximum(start, j * SLC_BLOCK)
    m = np.clip(ov, 0, None) / CMP_LEN
    m = np.where((i < n_cmp) & (j < n_slc), m, 0.0)
    return jnp.asarray(m, BF16)


def _gate_expand():
    e = np.zeros((A_GROUPS, LANES, 6 * LANES), np.float32)
    for g in range(A_GROUPS):
        for br in range(3):
            for c in range(2):
                for half in range(2):
                    h = 4 * g + 2 * c + half
                    col = (2 * br + c) * LANES + half * HEAD_DIM
                    e[g, MISC_GATE + br * A_HEADS + h, col:col + HEAD_DIM] = 1.0
    return jnp.asarray(e, BF16)


def _nsa(aq, misc, kc, vc, att, kw, vw, *, tq, pos0, n_cmp, n_slc):
    B, T, _ = aq.shape
    nch = kc.shape[2]
    Lk = att.shape[3]
    nsp = -(-max(n_slc, Lk // SLC_BLOCK) // LANES) * LANES
    m = _cmp_to_slc(nch, nsp, n_cmp, n_slc)
    tri = jnp.asarray(np.triu(np.ones((nsp, nsp)), 1), BF16)
    eg = _gate_expand()
    Lw = kw.shape[2]
    nt = T // tq
    in_specs = [
        pl.BlockSpec((None, tq, 256), lambda b, g, t: (b, t, g)),
        pl.BlockSpec((None, tq, LANES), lambda b, g, t: (b, t, 0)),
        _single((None, None, nch, LANES), lambda b, g, t: (b, g, 0, 0)),
        _single((None, None, nch, LANES), lambda b, g, t: (b, g, 0, 0)),
        _single((None, None, None, Lk, LANES), lambda b, g, t: (b, 0, g, 0, 0)),
        _single((None, None, None, Lk, LANES), lambda b, g, t: (b, 1, g, 0, 0)),
        _single((None, None, Lw, LANES), lambda b, g, t: (b, g, 0, 0)),
        _single((None, None, Lw, LANES), lambda b, g, t: (b, g, 0, 0)),
        _single(m.shape, lambda b, g, t: (0, 0)),
        _single(tri.shape, lambda b, g, t: (0, 0)),
        _single((None, LANES, 6 * LANES), lambda b, g, t: (g, 0, 0)),
    ]
    kern = functools.partial(_nsa_kernel, tq=tq, nt=nt, pos0=pos0, n_cmp=n_cmp, n_slc=n_slc)
    return pl.pallas_call(
        kern, grid=(B, A_GROUPS, nt), in_specs=in_specs,
        out_specs=pl.BlockSpec((None, tq, 256), lambda b, g, t: (b, t, g)),
        out_shape=jax.ShapeDtypeStruct((B, T, 512), F32),
        scratch_shapes=[pltpu.VMEM((nsp // LANES, tq, LANES), F32)],
        compiler_params=_cparams(("parallel", "parallel", "arbitrary")), name="nsa",
    )(aq, misc, kc, vc, att, att, kw, vw, m, tri, eg)


def _dsa_kernel(iq_ref, bq_ref, misc_ref, ik_ref, bk_ref, bv_ref, tri_ref, o_ref, key_ref, *, tq, pos0, k_sel):
    t0 = pos0 + pl.program_id(1) * tq
    pos = t0 + _row((tq, 1))
    nkt = t0 // KEY_TILE + 1
    reps = KEY_TILE // LANES
    qi4 = _heads_rows(iq_ref[...], 2)
    misc = misc_ref[...]
    wts = [misc[:, MISC_IW + h:MISC_IW + h + 1] for h in range(4)]
    ones = jnp.ones((LANES, LANES), BF16)
    lane_t = _lane((tq, KEY_TILE))

    def score_step(kt, _):
        off = pl.multiple_of(kt * KEY_TILE, KEY_TILE)
        d = jnp.maximum(_dot_nt(qi4, ik_ref[pl.ds(off, KEY_TILE), :]), 0.0)
        sc = d[0:tq] * wts[0] + d[tq:2 * tq] * wts[1] + d[2 * tq:3 * tq] * wts[2] + d[3 * tq:4 * tq] * wts[3]
        key_ref[kt] = _order_key(jnp.where(off + lane_t <= pos, sc, NEG_INF))
        return 0

    lax.fori_loop(0, nkt, score_step, 0)

    def count(pred, ref_vals):
        wides = [jnp.concatenate([v] * reps, axis=1) for v in ref_vals]

        def body(kt, parts):
            key = key_ref[kt]
            out = []
            for c, wide in zip(parts, wides):
                hit = jnp.where(pred(key, wide), 1.0, 0.0)
                for r in range(reps):
                    c = c + hit[:, r * LANES:(r + 1) * LANES]
                out.append(c)
            return tuple(out)
        parts = lax.fori_loop(0, nkt, body, tuple(jnp.zeros((tq, LANES), F32) for _ in ref_vals))
        return [_split_dot(p, ones) for p in parts]

    thr = _kth_largest(lambda cands: count(lambda k, w: k >= w, cands), float(k_sel), (tq, LANES))
    need = float(k_sel) - count(lambda k, w: k > w, [thr])[0]
    thr_w = jnp.concatenate([thr] * reps, axis=1)
    need_w = jnp.concatenate([need] * reps, axis=1)

    q8 = [_heads_rows(bq_ref[:, g * 256:(g + 1) * 256], 2) for g in range(B_KV_HEADS)]
    hg = B_HEADS // B_KV_HEADS
    r4 = hg * tq

    def step(kt, carry):
        ms, ls, accs, seen = carry
        off = pl.multiple_of(kt * KEY_TILE, KEY_TILE)
        key = key_ref[kt]
        eq = key == thr_w
        before = seen + _dot(jnp.where(eq, 1.0, 0.0).astype(BF16), tri_ref[...])
        sel = ((key > thr_w) | (eq & (before < need_w))) & (off + lane_t <= pos)
        bias = jnp.where(sel, 0.0, NEG_INF)[None]
        new_m, new_l, new_acc = [], [], []
        for g in range(B_KV_HEADS):
            sc = _dot_nt(q8[g], bk_ref[g, pl.ds(off, KEY_TILE), :]).reshape(hg, tq, KEY_TILE) + bias
            m_new = jnp.maximum(ms[g], jnp.max(sc, axis=-1, keepdims=True))
            alpha = jnp.exp(ms[g] - m_new)
            pe = jnp.exp(sc - m_new)
            new_m.append(m_new)
            new_l.append(alpha * ls[g] + jnp.sum(pe, axis=-1, keepdims=True))
            pv = _dot(pe.reshape(r4, KEY_TILE).astype(BF16), bv_ref[g, pl.ds(off, KEY_TILE), :])
            new_acc.append(alpha.reshape(r4, 1) * accs[g] + pv)
        seen = seen + jnp.sum(jnp.where(eq, 1.0, 0.0), axis=-1, keepdims=True)
        return tuple(new_m), tuple(new_l), tuple(new_acc), seen

    init = (tuple(jnp.full((hg, tq, 1), NEG_INF, F32) for _ in range(B_KV_HEADS)),
            tuple(jnp.zeros((hg, tq, 1), F32) for _ in range(B_KV_HEADS)),
            tuple(jnp.zeros((r4, LANES), F32) for _ in range(B_KV_HEADS)), jnp.zeros((tq, 1), F32))
    _, ls, accs, _ = lax.fori_loop(0, nkt, step, init)
    o = _pairs(jnp.concatenate([a / l.reshape(r4, 1) for a, l in zip(accs, ls)], axis=0), tq, 4)
    for c in range(4):
        o_ref[:, c * LANES:(c + 1) * LANES] = o[c]


def _dsa(iq, bq, misc, ika, batt, *, tq, pos0, k_sel):
    B, T, _ = bq.shape
    Lk = ika.shape[1]
    nt = T // tq
    tri = jnp.asarray(np.triu(np.ones((KEY_TILE, KEY_TILE)), 1), BF16)
    in_specs = [
        pl.BlockSpec((None, tq, 256), lambda b, t: (b, t, 0)),
        pl.BlockSpec((None, tq, 512), lambda b, t: (b, t, 0)),
        pl.BlockSpec((None, tq, LANES), lambda b, t: (b, t, 0)),
        _single((None, Lk, LANES), lambda b, t: (b, 0, 0)),
        _single((None, None, B_KV_HEADS, Lk, LANES), lambda b, t: (b, 0, 0, 0, 0)),
        _single((None, None, B_KV_HEADS, Lk, LANES), lambda b, t: (b, 1, 0, 0, 0)),
        _single(tri.shape, lambda b, t: (0, 0)),
    ]
    scratch = [pltpu.VMEM((Lk // KEY_TILE, tq, KEY_TILE), I32)]
    kern = functools.partial(_dsa_kernel, tq=tq, pos0=pos0, k_sel=k_sel)
    return pl.pallas_call(
        kern, grid=(B, nt), in_specs=in_specs,
        out_specs=pl.BlockSpec((None, tq, 512), lambda b, t: (b, t, 0)),
        out_shape=jax.ShapeDtypeStruct((B, T, 512), F32), scratch_shapes=scratch,
        compiler_params=_cparams(("parallel", "arbitrary")), name="dsa",
    )(iq, bq, misc, ika, batt, batt, tri)


GATHER_PAGES = 16


def _gather_kernel(pt_ref, cmpk_hbm, cmpv_hbm, slck_hbm, slcv_hbm, bk_hbm, bv_hbm, idx_hbm,
                   akv_ref, bkv_ref, misc_ref,
                   cmpk_ref, cmpv_ref, a_att_ref, b_att_ref, ika_ref,
                   b0, b1, b2, b3, b4, sem, *, nfull, ns):
    b, c = pl.program_id(0), pl.program_id(1)
    rows = GATHER_PAGES * PAGE_SIZE
    bufs = (b0, b1, b2, b3, b4)
    low = _lane((rows, LANES)) < HEAD_DIM

    def emit(xs):
        for i, (ref, s) in enumerate(((a_att_ref, 0), (a_att_ref, 1), (b_att_ref, 0), (b_att_ref, 1))):
            ref[s, 0] = jnp.where(low, xs[i], 0.0).astype(BF16)
            ref[s, 1] = jnp.where(low, pltpu.roll(xs[i], HEAD_DIM, 1), 0.0).astype(BF16)
        ika_ref[...] = jnp.where(low, xs[4], 0.0).astype(BF16)

    @pl.when(c < nfull)
    def _():
        copies = []
        for p in range(GATHER_PAGES):
            page = pt_ref[b, c * GATHER_PAGES + p]
            dst = pl.ds(p * PAGE_SIZE, PAGE_SIZE)
            copies.append(pltpu.make_async_copy(cmpk_hbm.at[page], cmpk_ref.at[0, dst], sem.at[0]))
            copies.append(pltpu.make_async_copy(cmpv_hbm.at[page], cmpv_ref.at[0, dst], sem.at[1]))
            for i, pool in enumerate((slck_hbm, slcv_hbm, bk_hbm, bv_hbm, idx_hbm)):
                copies.append(pltpu.make_async_copy(pool.at[page], bufs[i].at[dst], sem.at[2 + i]))
        for cp in copies:
            cp.start()
        for cp in copies:
            cp.wait()
        xi = b4[...]
        emit([b0[...], b1[...], b2[...], b3[...], jnp.concatenate([xi, jnp.zeros_like(xi)], axis=1)])

    @pl.when(c == nfull)
    def _():
        cmpk_ref[...] = jnp.zeros(cmpk_ref.shape, F32)
        cmpv_ref[...] = jnp.zeros(cmpv_ref.shape, F32)
        fill = jnp.zeros((rows - ns, LANES), F32)
        new = [akv_ref[2], akv_ref[3], bkv_ref[0], bkv_ref[1], misc_ref[...]]
        emit([jnp.concatenate([x, fill], axis=0) for x in new])


def _gather_pages(pools, page_table, kv, bkv, misc):
    DB, npg = page_table.shape
    ns = kv.shape[2]
    assert npg % GATHER_PAGES == 0
    nfull = npg // GATHER_PAGES
    rows = GATHER_PAGES * PAGE_SIZE
    L = (nfull + 1) * rows
    any_spec = pl.BlockSpec(memory_space=pl.ANY)
    new = lambda n: pl.BlockSpec((None, n, ns, LANES), lambda b, c, pt: (b, 0, 0, 0))
    raw = pl.BlockSpec((None, rows, LANES), lambda b, c, pt: (b, c, 0))
    dma_dst = pl.BlockSpec((1, rows, LANES), lambda b, c, pt: (b, c, 0))
    att = pl.BlockSpec((None, 2, A_GROUPS, rows, LANES), lambda b, c, pt: (b, 0, 0, c, 0))
    grid_spec = pltpu.PrefetchScalarGridSpec(
        num_scalar_prefetch=1, grid=(DB, nfull + 1),
        in_specs=[any_spec] * 7 + [new(6), new(2), pl.BlockSpec((None, ns, LANES), lambda b, c, pt: (b, 0, 0))],
        out_specs=[dma_dst, dma_dst, att, att, raw],
        scratch_shapes=[pltpu.VMEM((rows, LANES), F32)] * 4 + [pltpu.VMEM((rows, pools[6].shape[-1]), F32),
                                                                pltpu.SemaphoreType.DMA((7,))])
    out_shape = [jax.ShapeDtypeStruct((DB, L, LANES), F32), jax.ShapeDtypeStruct((DB, L, LANES), F32),
                 jax.ShapeDtypeStruct((DB, 2, A_GROUPS, L, LANES), BF16),
                 jax.ShapeDtypeStruct((DB, 2, B_KV_HEADS, L, LANES), BF16),
                 jax.ShapeDtypeStruct((DB, L, LANES), BF16)]
    return pl.pallas_call(
        functools.partial(_gather_kernel, nfull=nfull, ns=ns), grid_spec=grid_spec, out_shape=out_shape,
        compiler_params=_cparams(("parallel", "arbitrary")), name="gather_pages",
    )(page_table, *pools, kv, bkv, misc)


def _topk_rows(s, k, payload=None):
    n = s.shape[0]
    rows = _row(s.shape)
    vals, idxs = [], []
    for _ in range(k):
        m = jnp.max(s, axis=0, keepdims=True)
        idx = jnp.min(jnp.where(s == m, rows, n), axis=0, keepdims=True)
        hit = rows == idx
        vals.append(m)
        idxs.append(idx if payload is None else jnp.max(jnp.where(hit, payload, -1), axis=0, keepdims=True))
        s = jnp.where(hit, -jnp.inf, s)
    return jnp.concatenate(vals, axis=0), jnp.concatenate(idxs, axis=0)


def _route_kernel(x_ref, oa_ref, ob_ref, gates_ref, woa_ref, wob_ref, wo_ref, g2_ref, wq_ref, sub1_ref, sub2_ref,
                  h_ref, hn_ref, eid_ref, gate_ref):
    gates = gates_ref[...]
    ya = _dot(oa_ref[...].astype(BF16), woa_ref[...])
    yb = _dot(ob_ref[...].astype(BF16), wob_ref[...])
    mix = gates[:, :D_MODEL] * ya + gates[:, D_MODEL:] * yb
    h = x_ref[...] + _dot(mix.astype(BF16), wo_ref[...])
    h_ref[...] = h
    hn = h * lax.rsqrt(jnp.mean(h * h, axis=-1, keepdims=True) + EPS) * g2_ref[...]
    hn_ref[...] = hn
    q = _dot(hn.astype(BF16), wq_ref[...])
    half = PEER_KEY_DIM // 2
    for hd in range(PEER_HEADS):
        q1 = q[:, hd * PEER_KEY_DIM:hd * PEER_KEY_DIM + half].astype(BF16)
        q2 = q[:, hd * PEER_KEY_DIM + half:(hd + 1) * PEER_KEY_DIM].astype(BF16)
        v1, i1 = _topk_rows(_dot_nt(sub1_ref[hd], q1), PEER_TOPK)
        v2, i2 = _topk_rows(_dot_nt(sub2_ref[hd], q2), PEER_TOPK)
        cand = jnp.concatenate([v1[r:r + 1] + v2 for r in range(PEER_TOPK)], axis=0)
        cid = jnp.concatenate([i1[r:r + 1] * PEER_NKEYS + i2 for r in range(PEER_TOPK)], axis=0)
        vt, et = _topk_rows(cand, PEER_TOPK, payload=cid)
        e = jnp.exp(vt - vt[0:1])
        eid_ref[0, hd * PEER_TOPK:(hd + 1) * PEER_TOPK, :] = et
        gate_ref[0, hd * PEER_TOPK:(hd + 1) * PEER_TOPK, :] = e / jnp.sum(e, axis=0, keepdims=True)


def _route(x, oa, ob, gates, woa, wob, wo, g2, wq, sub1, sub2, tm):
    N, D = x.shape
    nt = N // tm
    tok = lambda c: pl.BlockSpec((tm, c), lambda i: (i, 0))
    const = lambda a: pl.BlockSpec(a.shape, lambda i: (0,) * a.ndim)
    slot = pl.BlockSpec((1, PEER_HEADS * PEER_TOPK, tm), lambda i: (i, 0, 0))
    consts = (woa, wob, wo, g2, wq, sub1, sub2)
    return pl.pallas_call(
        _route_kernel, grid=(nt,),
        in_specs=[tok(D), tok(512), tok(512), tok(2 * D)] + [const(a) for a in consts],
        out_specs=(tok(D), tok(D), slot, slot),
        out_shape=(jax.ShapeDtypeStruct((N, D), F32), jax.ShapeDtypeStruct((N, D), F32),
                   jax.ShapeDtypeStruct((nt, PEER_HEADS * PEER_TOPK, tm), I32),
                   jax.ShapeDtypeStruct((nt, PEER_HEADS * PEER_TOPK, tm), F32)),
        compiler_params=_cparams(("parallel",)), name="route",
    )(x, oa, ob, gates, *consts)


def _pack_table(t):
    E, D = t.shape
    b = lax.bitcast_convert_type(t.astype(BF16), jnp.uint16).astype(jnp.uint32)
    w = b[:, :D // 2] | (b[:, D // 2:] << 16)
    return lax.bitcast_convert_type(w, I32).reshape(E, D // 256, LANES)


def _unpack_row(w):
    return pltpu.bitcast(w << 16, F32), pltpu.bitcast(w & jnp.int32(-65536), F32)


def _load_table(tab_hbm, tab_vmem, sem):
    @pl.when(pl.program_id(0) == 0)
    def _():
        cp = pltpu.make_async_copy(tab_hbm, tab_vmem, sem)
        cp.start()
        cp.wait()


PEER_LANE_GROUP = 8
PEER_USES_PER_ROW = LANES // PEER_LANE_GROUP


def _peer_act_kernel(eid_ref, x_ref, gsum_ref, tab_hbm, act_ref, tab_vmem, sem, stage0_ref, stage1_ref, v_ref,
                     *, tt, n_slot):
    _load_table(tab_hbm, tab_vmem, sem)
    rows_per_use = stage0_ref.shape[0] // n_slot
    lane, sub = _lane((PEER_LANE_GROUP, LANES)), _row((PEER_LANE_GROUP, LANES))
    masks = [((lane >> 3) == c) & ((lane & 7) == sub) for c in range(PEER_USES_PER_ROW)]
    groups = n_slot // PEER_USES_PER_ROW
    vrows = groups * PEER_LANE_GROUP

    def one(t, stage_ref):
        base = t * n_slot
        for j in range(n_slot):
            stage_ref[rows_per_use * j:rows_per_use * (j + 1), :] = tab_vmem[eid_ref[base + j]]
        bm = pltpu.bitcast(stage_ref[...], BF16)
        xrep = jnp.concatenate([x_ref[t]] * PEER_USES_PER_ROW, axis=0).astype(BF16)
        g = _dot_nt(bm, xrep)
        vs = []
        for m in range(groups):
            v = jnp.zeros((PEER_LANE_GROUP, LANES), F32)
            for c in range(PEER_USES_PER_ROW):
                r0 = PEER_LANE_GROUP * (PEER_USES_PER_ROW * m + c)
                v = v + jnp.where(masks[c], g[r0:r0 + PEER_LANE_GROUP], 0.0)
            vs.append(v)
        v_ref[pl.ds(pl.multiple_of(t * vrows, vrows), vrows), :] = jnp.concatenate(vs, axis=0)

    def pair(i, _):
        one(2 * i, stage0_ref)
        one(2 * i + 1, stage1_ref)
        return 0

    lax.fori_loop(0, tt // 2, pair, 0)
    a = _split_dot(v_ref[...], gsum_ref[...])
    act_ref[...] = jnp.sum(a.reshape(tt * groups, PEER_LANE_GROUP, LANES), axis=1).reshape(tt, groups, LANES)


def _gelu_gate_kernel(act_ref, gate_ref, w_ref):
    act = act_ref[...]
    w_ref[...] = gate_ref[...] * (act * (lax.erf(act * (2.0 ** -0.5)) + 1.0) * 0.5)


def _peer_out_kernel(eid_ref, w_ref, h_ref, tab_hbm, y_ref, tab_vmem, sem, *, tt, n_slot):
    _load_table(tab_hbm, tab_vmem, sem)
    half = h_ref.shape[1] // 2
    chunk = 16

    def token(t, _):
        def part(c, acc):
            acc = list(acc)
            base = t * n_slot + c * chunk
            for j in range(chunk):
                lo, hi = _unpack_row(tab_vmem[eid_ref[base + j]])
                wj = w_ref[base + j]
                acc[2 * (j % 2)] = acc[2 * (j % 2)] + wj * lo
                acc[2 * (j % 2) + 1] = acc[2 * (j % 2) + 1] + wj * hi
            return tuple(acc)

        acc = lax.fori_loop(0, n_slot // chunk, part, tuple(jnp.zeros((half, LANES), F32) for _ in range(4)))
        y_ref[t] = h_ref[t] + jnp.concatenate([acc[0] + acc[2], acc[1] + acc[3]], axis=0)
        return 0

    lax.fori_loop(0, tt, token, 0)


def _peer(eid, gate, hn, h, u_tab, v_tab, tt):
    nt, n_slot, _ = eid.shape
    N, D = h.shape
    sub = D // LANES
    assert n_slot == LANES and sub == PEER_LANE_GROUP
    eid_flat = eid.transpose(0, 2, 1).reshape(-1)
    gate_t = gate.transpose(0, 2, 1).reshape(N, n_slot)
    xr = hn.reshape(N, 2, sub // 2, LANES).transpose(0, 2, 1, 3).reshape(N, sub, LANES)
    gsum = jnp.asarray((np.arange(LANES)[:, None] // PEER_LANE_GROUP == np.arange(LANES)[None, :]), BF16)
    smem = lambda: pl.BlockSpec((n_slot * tt,), lambda i: (i,), memory_space=pltpu.SMEM)
    rows = pl.BlockSpec((tt, sub, LANES), lambda i: (i, 0, 0))
    any_spec = pl.BlockSpec(memory_space=pl.ANY)
    tab_scratch = [pltpu.VMEM(u_tab.shape, I32), pltpu.SemaphoreType.DMA(())]
    act = pl.pallas_call(
        functools.partial(_peer_act_kernel, tt=tt, n_slot=n_slot), grid=(nt,),
        in_specs=[smem(), rows, pl.BlockSpec((LANES, LANES), lambda i: (0, 0)), any_spec], out_specs=rows,
        out_shape=jax.ShapeDtypeStruct((N, sub, LANES), F32),
        scratch_shapes=tab_scratch + [pltpu.VMEM((n_slot * u_tab.shape[1], LANES), I32)] * 2
        + [pltpu.VMEM((tt * n_slot // PEER_USES_PER_ROW * PEER_LANE_GROUP, LANES), F32)],
        compiler_params=_cparams(("arbitrary",)), name="peer_act",
    )(eid_flat, xr, gsum, u_tab)
    act_t = act[:, :, :PEER_USES_PER_ROW].reshape(N, n_slot)
    tw = 1024 if N % 1024 == 0 else N
    tile = pl.BlockSpec((tw, n_slot), lambda i: (i, 0))
    w = pl.pallas_call(
        _gelu_gate_kernel, grid=(N // tw,), in_specs=[tile, tile], out_specs=tile,
        out_shape=jax.ShapeDtypeStruct((N, n_slot), F32),
        compiler_params=_cparams(("parallel",)), name="gelu_gate",
    )(act_t, gate_t)
    y = pl.pallas_call(
        functools.partial(_peer_out_kernel, tt=tt, n_slot=n_slot), grid=(nt,),
        in_specs=[smem(), smem(), rows, any_spec], out_specs=rows,
        out_shape=jax.ShapeDtypeStruct((N, sub, LANES), F32), scratch_shapes=tab_scratch,
        compiler_params=_cparams(("arbitrary",)), name="peer_out",
    )(eid_flat, w.reshape(-1), h.reshape(N, sub, LANES), v_tab)
    return y.reshape(N, D)


def _relayout_w_in(w_in):
    names = (('a_q', 512), ('a_k_cmp', 128), ('a_v_cmp', 128), ('a_k_slc', 128), ('a_v_slc', 128),
             ('a_k_win', 128), ('a_v_win', 128), ('a_gate', 24), ('b_q', 512), ('b_k', 128), ('b_v', 128),
             ('i_q', 256), ('i_w', 4), ('i_k', 64), ('merge', 2 * D_MODEL))
    sec, o = {}, 0
    for name, n in names:
        sec[name] = w_in[:, o:o + n]
        o += n
    pad = jnp.zeros((w_in.shape[0], LANES - 64 - 4 - 24), w_in.dtype)
    order = ['a_q', 'a_k_cmp', 'a_v_cmp', 'a_k_slc', 'a_v_slc', 'a_k_win', 'a_v_win', 'b_q', 'b_k', 'b_v',
             'i_q', 'i_k', 'i_w', 'a_gate']
    return jnp.concatenate([sec[n] for n in order] + [pad, sec['merge']], axis=1).astype(BF16)


def _group_lanes(x):
    z = jnp.zeros_like(x[..., :HEAD_DIM])
    parts = [jnp.concatenate([x[..., g * HEAD_DIM:(g + 1) * HEAD_DIM], z], axis=-1) for g in range(2)]
    return jnp.stack(parts, axis=-3).astype(BF16)


def _pad_rows(x, total, axis):
    pad = [(0, 0)] * x.ndim
    pad[axis] = (0, total - x.shape[axis])
    return jnp.pad(x, pad)


def kernel(x_prompt, x_sample, cache_a_cmp_k, cache_a_cmp_v, cache_a_slc_k, cache_a_slc_v, state_a_win_k, state_a_win_v, cache_b_k, cache_b_v, cache_b_idx_k, page_table, norm1_g, w_in, g_qa, g_ka_cmp, g_ka_slc, g_ka_win, g_qb, g_kb, g_kidx, w_cmp_k, pe_cmp_k, w_cmp_v, pe_cmp_v, g_kc, w_out_a, w_out_b, w_o, norm2_g, w_peer_q, peer_subkeys, peer_u, peer_v):
    B, S, D = x_prompt.shape
    DB, NS, _ = x_sample.shape
    npg = page_table.shape[1]
    past = npg * PAGE_SIZE
    assert D == D_MODEL and S % KEY_TILE == 0 and past % KEY_TILE == 0 and NS < CMP_STRIDE
    assert state_a_win_k.shape[1] == WINDOW

    wp = _relayout_w_in(w_in)
    ones = jnp.ones((HEAD_DIM,), F32)
    gains = jnp.stack([jnp.tile(g, 2) for g in (g_qa, g_ka_cmp, g_ka_slc, g_ka_win, g_qb, g_kb)]
                      + [jnp.concatenate([g_kidx, ones]), jnp.tile(ones, 2)])
    wk2, pek2 = _compress_weights(w_cmp_k, pe_cmp_k)
    wv2, pev2 = _compress_weights(w_cmp_v, pe_cmp_v)
    woa, wob, wo = w_out_a.astype(BF16), w_out_b.astype(BF16), w_o.astype(BF16)
    wq = w_peer_q.reshape(D, PEER_HEADS * PEER_KEY_DIM).astype(BF16)
    sub1, sub2 = peer_subkeys[0].astype(BF16), peer_subkeys[1].astype(BF16)
    g2 = norm2_g.reshape(1, D)
    u_tab, v_tab = _pack_table(peer_u), _pack_table(peer_v)

    def finish(x, oa, ob, gates):
        n = x.shape[0] * x.shape[1]
        tt = 128 if n % 128 == 0 else n
        h, hn, eid, gate = _route(x.reshape(n, D), oa.reshape(n, 512), ob.reshape(n, 512),
                                  gates.reshape(n, 2 * D), woa, wob, wo, g2, wq, sub1, sub2, tt)
        return _peer(eid, gate, hn, h, u_tab, v_tab, tt).reshape(x.shape)

    def heads(a, n):
        return a.reshape(a.shape[0], a.shape[1], n, HEAD_DIM)

    total = past + NS
    aq, bq, iq, kv, bkv, misc, ika, gates, a_att, b_att = _project(
        x_sample, past + jnp.arange(NS), norm1_g, wp, gains, NS)
    flat = lambda c: c.reshape(c.shape[0], PAGE_SIZE, -1)
    cmpk_s, cmpv_s, a_att_s, b_att_s, ika_s = _gather_pages(
        [flat(c) for c in (cache_a_cmp_k, cache_a_cmp_v, cache_a_slc_k, cache_a_slc_v, cache_b_k, cache_b_v,
                           cache_b_idx_k)], page_table, kv, bkv, misc)
    nch_s = total // CMP_STRIDE
    assert nch_s * CMP_STRIDE <= past
    kc, vc = _compress((cmpk_s, 0), (cmpv_s, 0), nch_s, wk2, wv2, pek2, pev2, g_kc)
    wl = -(-(WINDOW + NS) // LANES) * LANES
    kw_s = _pad_rows(jnp.concatenate([_group_lanes(state_a_win_k.reshape(DB, WINDOW, LANES)), a_att[:, 2]], axis=2), wl, 2)
    vw_s = _pad_rows(jnp.concatenate([_group_lanes(state_a_win_v.reshape(DB, WINDOW, LANES)), a_att[:, 3]], axis=2), wl, 2)
    o_a = _nsa(aq, misc, kc, vc, a_att_s, kw_s, vw_s, tq=NS, pos0=past, n_cmp=nch_s - 1,
               n_slc=-(-total // SLC_BLOCK))
    o_b = _dsa(iq, bq, misc, ika_s, b_att_s, tq=NS, pos0=past, k_sel=min(DSA_TOPK_MAX, total // 4))
    y_sample = finish(x_sample, o_a, o_b, gates)
    s_out = [heads(kv[:, i], A_GROUPS) for i in range(4)]
    s_out += [jnp.concatenate([state_a_win_k[:, NS:], heads(kv[:, 4], A_GROUPS)], axis=1),
              jnp.concatenate([state_a_win_v[:, NS:], heads(kv[:, 5], A_GROUPS)], axis=1)]
    s_out += [heads(bkv[:, 0], B_KV_HEADS), heads(bkv[:, 1], B_KV_HEADS), misc[:, :, :IDX_DIM]]

    tq = 128
    aq, bq, iq, kv, bkv, misc, ika, gates, a_att, b_att = _project(
        x_prompt, jnp.arange(S), norm1_g, wp, gains, 256)
    nch = S // CMP_STRIDE
    kc, vc = _compress((kv, 0), (kv, 1), nch, wk2, wv2, pek2, pev2, g_kc)
    front = ((0, 0), (0, 0), (WINDOW, 0), (0, 0))
    o_a = _nsa(aq, misc, kc, vc, a_att, jnp.pad(a_att[:, 2], front), jnp.pad(a_att[:, 3], front),
               tq=tq, pos0=0, n_cmp=nch - 1, n_slc=-(-S // SLC_BLOCK))
    o_b = _dsa(iq, bq, misc, ika, b_att, tq=tq, pos0=0, k_sel=min(DSA_TOPK_MAX, S // 4))
    y_prompt = finish(x_prompt, o_a, o_b, gates)
    win_p = min(WINDOW, S)
    p_out = [heads(kv[:, i], A_GROUPS) for i in range(4)]
    p_out += [heads(kv[:, i, S - win_p:], A_GROUPS) for i in (4, 5)]
    p_out += [heads(bkv[:, 0], B_KV_HEADS), heads(bkv[:, 1], B_KV_HEADS), misc[:, :, :IDX_DIM]]
    return (y_prompt, y_sample, *p_out, *s_out)
```

```python
import functools

import numpy as np
import jax
import jax.numpy as jnp
from jax import lax
from jax.experimental import pallas as pl
from jax.experimental.pallas import tpu as pltpu

F32 = jnp.float32
BF16 = jnp.bfloat16
I32 = jnp.int32

LANES = 128
HEAD_DIM = 64
ROPE_DIM = HEAD_DIM // 4
ROPE_THETA = 500000.0
QK_SCALE = HEAD_DIM ** -0.5
EPS = 1e-6
NEG_INF = -1e30
A_HEADS = 8
A_GROUPS = 2
CMP_STRIDE = 16
CMP_LEN = 32
SLC_BLOCK = 64
SLC_TOP = 16
FORCE_BONUS = 1e4
WINDOW = 512
B_HEADS = 8
B_KV_HEADS = 2
IDX_HEADS = 4
IDX_DIM = 64
IDX_SCALE = (IDX_HEADS * IDX_DIM) ** -0.5
DSA_TOPK_MAX = 256
PEER_HEADS = 8
PEER_NKEYS = 128
PEER_KEY_DIM = 256
PEER_TOPK = 16
PAGE_SIZE = 128
D_MODEL = 1024

KEY_TILE = 512
VMEM_LIMIT = 56 * 1024 * 1024

OFF_AQ, OFF_KV, OFF_BQ, OFF_BKV, OFF_IQ, OFF_MISC, OFF_MERGE = 0, 512, 1280, 1792, 2048, 2304, 2432
D_PROJ = OFF_MERGE + 2 * D_MODEL
MISC_IW = 64
MISC_GATE = 68

_NT = (((1,), (1,)), ((), ()))


def _dot(a, b):
    return jnp.dot(a, b, preferred_element_type=F32)


def _dot_nt(a, b):
    return lax.dot_general(a, b, _NT, preferred_element_type=F32)


def _split_dot(x, w):
    hi = x.astype(BF16)
    lo = (x - hi.astype(F32)).astype(BF16)
    return _dot(hi, w) + _dot(lo, w)


def _lane(shape):
    return lax.broadcasted_iota(I32, shape, len(shape) - 1)


def _row(shape):
    return lax.broadcasted_iota(I32, shape, len(shape) - 2)


def _order_key(x):
    b = pltpu.bitcast(x + 0.0, I32)
    return b ^ ((b >> 31) & jnp.int32(0x7FFFFFFF))


def _cparams(sem):
    return pltpu.CompilerParams(dimension_semantics=sem, vmem_limit_bytes=VMEM_LIMIT)


def _single(shape, index_map):
    return pl.BlockSpec(shape, index_map, pipeline_mode=pl.Buffered(1))


def _proj_kernel(x_ref, g1_ref, w_ref, gains_ref, bd_ref, rc_ref, rs1_ref, rs2_ref,
                 aq_ref, bq_ref, iq_ref, kv_ref, bkv_ref, misc_ref, ika_ref, gates_ref,
                 akv_att_ref, bkv_att_ref):
    x = x_ref[0]
    xn = x * lax.rsqrt(jnp.mean(x * x, axis=-1, keepdims=True) + EPS) * g1_ref[...]
    xb = xn.astype(BF16)
    bd = bd_ref[...]
    rc, rs1, rs2 = rc_ref[...], rs1_ref[...], rs2_ref[...]
    lane = _lane((x.shape[0], LANES))
    low = lane < HEAD_DIM

    def proj(a, b):
        return _dot(xb, w_ref[:, a:b])

    def norm(y, g):
        return y * lax.rsqrt(_split_dot(y * y, bd) * (1.0 / HEAD_DIM) + EPS) * g

    def rope(y):
        return y * rc + pltpu.roll(y, LANES - ROPE_DIM // 2, 1) * rs1 + pltpu.roll(y, ROPE_DIM // 2, 1) * rs2

    def att_copy(ref, s, y):
        ref[0, s, 0] = jnp.where(low, y, 0.0).astype(BF16)
        ref[0, s, 1] = jnp.where(low, pltpu.roll(y, HEAD_DIM, 1), 0.0).astype(BF16)

    z = proj(OFF_AQ, OFF_AQ + 512)
    for c in range(4):
        aq_ref[0, :, c * LANES:(c + 1) * LANES] = rope(norm(z[:, c * LANES:(c + 1) * LANES], gains_ref[0:1])) * QK_SCALE
    z = proj(OFF_KV, OFF_KV + 768)
    for s in range(6):
        y = z[:, s * LANES:(s + 1) * LANES]
        if s % 2 == 0:
            y = rope(norm(y, gains_ref[1 + s // 2:2 + s // 2]))
        kv_ref[0, s] = y
        if s >= 2:
            att_copy(akv_att_ref, s - 2, y)
    z = proj(OFF_BQ, OFF_BQ + 512)
    for c in range(4):
        bq_ref[0, :, c * LANES:(c + 1) * LANES] = rope(norm(z[:, c * LANES:(c + 1) * LANES], gains_ref[4:5])) * QK_SCALE
    z = proj(OFF_BKV, OFF_BKV + 256)
    yk = rope(norm(z[:, :LANES], gains_ref[5:6]))
    yv = z[:, LANES:]
    bkv_ref[0, 0] = yk
    bkv_ref[0, 1] = yv
    att_copy(bkv_att_ref, 0, yk)
    att_copy(bkv_att_ref, 1, yv)
    z = proj(OFF_IQ, OFF_IQ + 256)
    for c in range(2):
        iq_ref[0, :, c * LANES:(c + 1) * LANES] = rope(z[:, c * LANES:(c + 1) * LANES])
    z = proj(OFF_MISC, OFF_MISC + LANES)
    ik = rope(norm(z, gains_ref[6:7]))
    misc_ref[0] = jnp.where(low, ik, jnp.where(lane < MISC_GATE, z * IDX_SCALE, jax.nn.sigmoid(z)))
    ika_ref[0] = jnp.where(low, ik, 0.0).astype(BF16)
    gates_ref[0] = jax.nn.sigmoid(proj(OFF_MERGE, OFF_MERGE + 2 * D_MODEL))


def _project(x, pos, norm1_g, wp, gains, tm):
    B, T, D = x.shape
    half = ROPE_DIM // 2
    inv = jnp.power(ROPE_THETA, -jnp.arange(half, dtype=F32) / half)
    ang = pos.astype(F32)[:, None] * inv[None, :]
    cos, sin = jnp.cos(ang), jnp.sin(ang)
    one = jnp.ones((T, HEAD_DIM - ROPE_DIM), F32)
    zero = jnp.zeros((T, HEAD_DIM - ROPE_DIM), F32)
    zh = jnp.zeros((T, half), F32)
    rc = jnp.tile(jnp.concatenate([cos, cos, one], axis=1), (1, 2))
    rs1 = jnp.tile(jnp.concatenate([-sin, zh, zero], axis=1), (1, 2))
    rs2 = jnp.tile(jnp.concatenate([zh, sin, zero], axis=1), (1, 2))
    bd = jnp.asarray(np.kron(np.eye(2), np.ones((HEAD_DIM, HEAD_DIM))), BF16)
    nt = T // tm
    tok = lambda c: pl.BlockSpec((1, tm, c), lambda b, t: (b, t, 0))
    const = lambda shape: pl.BlockSpec(shape, lambda b, t: (0,) * len(shape))
    tab = pl.BlockSpec((tm, LANES), lambda b, t: (t, 0))
    stack = lambda n: pl.BlockSpec((1, n, tm, LANES), lambda b, t: (b, 0, t, 0))
    att = lambda n: pl.BlockSpec((1, n, A_GROUPS, tm, LANES), lambda b, t: (b, 0, 0, t, 0))
    out_shape = (
        jax.ShapeDtypeStruct((B, T, 512), F32),
        jax.ShapeDtypeStruct((B, T, 512), F32),
        jax.ShapeDtypeStruct((B, T, 256), F32),
        jax.ShapeDtypeStruct((B, 6, T, LANES), F32),
        jax.ShapeDtypeStruct((B, 2, T, LANES), F32),
        jax.ShapeDtypeStruct((B, T, LANES), F32),
        jax.ShapeDtypeStruct((B, T, LANES), BF16),
        jax.ShapeDtypeStruct((B, T, 2 * D_MODEL), F32),
        jax.ShapeDtypeStruct((B, 4, A_GROUPS, T, LANES), BF16),
        jax.ShapeDtypeStruct((B, 2, B_KV_HEADS, T, LANES), BF16),
    )
    out_specs = (tok(512), tok(512), tok(256), stack(6), stack(2), tok(LANES), tok(LANES),
                 tok(2 * D_MODEL), att(4), att(2))
    return pl.pallas_call(
        _proj_kernel, grid=(B, nt), out_shape=out_shape, out_specs=out_specs,
        in_specs=[tok(D), const((1, D)), const((D, D_PROJ)), const((8, LANES)), const((LANES, LANES)),
                  tab, tab, tab],
        compiler_params=_cparams(("parallel", "parallel")), name="proj",
    )(x, norm1_g.reshape(1, D), wp, gains, bd, rc, rs1, rs2)


def _compress_kernel(k_ref, v_ref, wk_ref, wv_ref, pek_ref, pev_ref, gkc_ref, kc_ref, vc_ref, *, nch):
    def one(src_ref, w_ref, pe_ref, out_ref, gain):
        r = None
        for j in range(CMP_STRIDE):
            d = _dot(src_ref[pl.ds(j, nch, stride=CMP_STRIDE), :].astype(BF16), w_ref[j * LANES:(j + 1) * LANES, :])
            r = d if r is None else r + d
        rpe = _dot(pe_ref[...].astype(BF16), w_ref[...])
        for g in range(A_GROUPS):
            lo = r[:, g * LANES:(g + 1) * LANES]
            hi = r[:, (2 + g) * LANES:(3 + g) * LANES]
            c = rpe[0:1, g * LANES:(g + 1) * LANES] + rpe[1:2, (2 + g) * LANES:(3 + g) * LANES]
            y = lo + pltpu.roll(hi, nch - 1, 0) + c
            if gain is not None:
                ms = jnp.sum(y * y, axis=-1, keepdims=True) * (1.0 / HEAD_DIM)
                y = y * lax.rsqrt(ms + EPS) * gain
            out_ref[0, g] = y.astype(BF16)

    one(k_ref, wk_ref, pek_ref, kc_ref, gkc_ref[...])
    one(v_ref, wv_ref, pev_ref, vc_ref, None)


def _compress_weights(w_c, pe):
    half = CMP_STRIDE
    w2 = jnp.zeros((half, A_GROUPS, HEAD_DIM, 4, LANES), F32)
    for g in range(A_GROUPS):
        w2 = w2.at[:, g, :, g, :HEAD_DIM].set(w_c[:half])
        w2 = w2.at[:, g, :, 2 + g, :HEAD_DIM].set(w_c[half:])
    w2 = w2.reshape(half * A_GROUPS * HEAD_DIM, 4 * LANES).astype(BF16)
    pe_lo = jnp.tile(pe[:half, None, :], (1, A_GROUPS, 1)).reshape(1, -1)
    pe_hi = jnp.tile(pe[half:, None, :], (1, A_GROUPS, 1)).reshape(1, -1)
    pe2 = jnp.concatenate([pe_lo, pe_hi, jnp.zeros((6, pe_lo.shape[1]), F32)], axis=0)
    return w2, pe2


def _compress(k_src, v_src, nch, wk2, wv2, pek2, pev2, g_kc):
    B = k_src[0].shape[0]
    gkc = jnp.concatenate([g_kc, jnp.zeros((LANES - HEAD_DIM,), F32)]).reshape(1, LANES)

    def src(a, i):
        if a.ndim == 4:
            return pl.BlockSpec((None, None, nch * CMP_STRIDE, LANES), lambda b: (b, i, 0, 0))
        return pl.BlockSpec((None, nch * CMP_STRIDE, LANES), lambda b: (b, 0, 0))
    const = lambda shape: pl.BlockSpec(shape, lambda b: (0,) * len(shape))
    out = pl.BlockSpec((1, A_GROUPS, nch, LANES), lambda b: (b, 0, 0, 0))
    shp = jax.ShapeDtypeStruct((B, A_GROUPS, nch, LANES), BF16)
    return pl.pallas_call(
        functools.partial(_compress_kernel, nch=nch), grid=(B,), out_shape=(shp, shp), out_specs=(out, out),
        in_specs=[src(*k_src), src(*v_src), const(wk2.shape), const(wv2.shape), const(pek2.shape),
                  const(pev2.shape), const((1, LANES))],
        compiler_params=_cparams(("parallel",)), name="compress",
    )(k_src[0], v_src[0], wk2, wv2, pek2, pev2, gkc)


def _heads_rows(q, n_chunks):
    rows = []
    for c in range(n_chunks):
        ch = q[:, c * LANES:(c + 1) * LANES]
        rows.append(ch.astype(BF16))
        rows.append(pltpu.roll(ch, HEAD_DIM, 1).astype(BF16))
    return jnp.concatenate(rows, axis=0)


def _pairs(o, t, n_chunks):
    return [o[(2 * c) * t:(2 * c + 1) * t] + pltpu.roll(o[(2 * c + 1) * t:(2 * c + 2) * t], HEAD_DIM, 1)
            for c in range(n_chunks)]


def _kth_largest(count_ge, k, shape, bits=1):
    def body(i, t):
        step = lax.shift_left(jnp.int32(1), jnp.int32(32 - bits) - bits * i)
        counts = count_ge([t + step * r for r in range(1, 1 << bits)])
        inc = jnp.zeros(shape, I32)
        for c in counts:
            inc = inc + jnp.where(c >= k, 1, 0)
        return t + inc * step
    return lax.fori_loop(0, 32 // bits, body, jnp.full(shape, jnp.iinfo(jnp.int32).min, I32))


def _softmax_rows(s):
    m = jnp.max(s, axis=-1, keepdims=True)
    e = jnp.exp(s - m)
    return e / jnp.sum(e, axis=-1, keepdims=True)


def _nsa_kernel(q_ref, misc_ref, kc_ref, vc_ref, ks_ref, vs_ref, kw_ref, vw_ref, m_ref, tri_ref, eg_ref,
                o_ref, selb_ref, *, tq, nt, pos0, n_cmp, n_slc):
    nch = kc_ref.shape[0]
    nsp = m_ref.shape[1]
    ncs = nsp // LANES
    t0 = pos0 + pl.program_id(2) * tq
    pos = t0 + _row((tq, 1))
    pos4 = jnp.concatenate([pos] * 4, axis=0)
    q4 = _heads_rows(q_ref[...], 2)
    ones = jnp.ones((LANES, LANES), BF16)

    s = _dot_nt(q4, kc_ref[...])
    blk_c = _lane((4 * tq, nch))
    s = jnp.where((blk_c * CMP_STRIDE + (CMP_LEN - 1) <= pos4) & (blk_c < n_cmp), s, NEG_INF)
    p = jnp.where(pos4 >= CMP_LEN - 1, _softmax_rows(s), 0.0)
    o_cmp = _dot(p.astype(BF16), vc_ref[...])
    psum = p[0:tq] + p[tq:2 * tq] + p[2 * tq:3 * tq] + p[3 * tq:4 * tq]
    imp = _split_dot(psum, m_ref[...])

    blk = _lane((tq, nsp))
    cur = pos >> 6
    forced = (blk == 0) | (blk == cur) | (blk == cur - 1)
    adm = (blk * SLC_BLOCK <= pos) & (blk < n_slc)
    key = _order_key(jnp.where(adm, imp + jnp.where(forced, FORCE_BONUS, 0.0), NEG_INF))
    keys = [key[:, c * LANES:(c + 1) * LANES] for c in range(ncs)]

    def count(pred):
        return sum(_dot(jnp.where(pred(kc), 1.0, 0.0).astype(BF16), ones) for kc in keys)

    thr = _kth_largest(lambda cands: [count(lambda kc: kc >= cand) for cand in cands], float(SLC_TOP),
                       (tq, LANES), bits=4)
    need = float(SLC_TOP) - count(lambda kc: kc > thr)
    thr_w = jnp.concatenate([thr] * ncs, axis=1)
    need_w = jnp.concatenate([need] * ncs, axis=1)
    eq = key == thr_w
    before = _dot(jnp.where(eq, 1.0, 0.0).astype(BF16), tri_ref[...])
    sel = ((key > thr_w) | (eq & (before < need_w))) & adm
    selb = jnp.where(sel, 0.0, NEG_INF)
    for c in range(ncs):
        selb_ref[c] = selb[:, c * LANES:(c + 1) * LANES]

    klast = t0 // KEY_TILE
    blocks_per_tile = KEY_TILE // SLC_BLOCK
    tiles_per_chunk = LANES // blocks_per_tile

    def augment(chunk):
        sb = selb_ref[chunk].astype(BF16)
        return jnp.concatenate([q4, jnp.concatenate([sb] * 4, axis=0)], axis=1)

    qa_all = augment(0) if ncs == 1 else None

    def step(kt, carry, causal):
        m, l, acc = carry
        qa = qa_all if ncs == 1 else augment(kt // tiles_per_chunk)
        off = pl.multiple_of(kt * KEY_TILE, KEY_TILE)
        kblk = (kt % tiles_per_chunk) * blocks_per_tile + (_row((KEY_TILE, LANES)) >> 6)
        et = jnp.where(kblk == _lane((KEY_TILE, LANES)), 1.0, 0.0).astype(BF16)
        ka = jnp.concatenate([ks_ref[pl.ds(off, KEY_TILE), :], et], axis=1)
        sc = _dot_nt(qa, ka)
        if causal:
            sc = jnp.where(off + _lane(sc.shape) <= pos4, sc, NEG_INF)
        m_new = jnp.maximum(m, jnp.max(sc, axis=-1, keepdims=True))
        alpha = jnp.exp(m - m_new)
        pe = jnp.exp(sc - m_new)
        l = alpha * l + jnp.sum(pe, axis=-1, keepdims=True)
        acc = alpha * acc + _dot(pe.astype(BF16), vs_ref[pl.ds(off, KEY_TILE), :])
        return m_new, l, acc

    init = (jnp.full((4 * tq, 1), NEG_INF, F32), jnp.zeros((4 * tq, 1), F32), jnp.zeros((4 * tq, LANES), F32))
    carry = lax.fori_loop(0, klast, lambda kt, c: step(kt, c, False), init)
    _, l, acc = step(klast, carry, True)
    o_slc = acc / l

    wl = kw_ref.shape[0] if nt == 1 else WINDOW + tq
    if nt == 1:
        kw, vw = kw_ref[...], vw_ref[...]
    else:
        ws = pl.multiple_of(pl.program_id(2) * tq, tq)
        kw, vw = kw_ref[pl.ds(ws, wl), :], vw_ref[pl.ds(ws, wl), :]
    sw = _dot_nt(q4, kw)
    pos_w = t0 - WINDOW + _lane(sw.shape)
    rel = pos4 - pos_w
    sw = jnp.where((rel >= 0) & (rel <= WINDOW) & (pos_w >= 0), sw, NEG_INF)
    o_win = _dot(_softmax_rows(sw).astype(BF16), vw)

    gates = _split_dot(misc_ref[...], eg_ref[...])
    pc, ps, pw = _pairs(o_cmp, tq, 2), _pairs(o_slc, tq, 2), _pairs(o_win, tq, 2)
    for c in range(2):
        g = lambda br: gates[:, (2 * br + c) * LANES:(2 * br + c + 1) * LANES]
        o_ref[:, c * LANES:(c + 1) * LANES] = g(0) * pc[c] + g(1) * ps[c] + g(2) * pw[c]


def _cmp_to_slc(nch, nsp, n_cmp, n_slc):
    i = np.arange(nch)[:, None]
    j = np.arange(nsp)[None, :]
    start = i * CMP_STRIDE
    ov = np.minimum(start + CMP_LEN, (j + 1) * SLC_BLOCK) - np.maximum(start, j * SLC_BLOCK)
    m = np.clip(ov, 0, None) / CMP_LEN
    m = np.where((i < n_cmp) & (j < n_slc), m, 0.0)
    return jnp.asarray(m, BF16)


def _gate_expand():
    e = np.zeros((A_GROUPS, LANES, 6 * LANES), np.float32)
    for g in range(A_GROUPS):
        for br in range(3):
            for c in range(2):
                for half in range(2):
                    h = 4 * g + 2 * c + half
                    col = (2 * br + c) * LANES + half * HEAD_DIM
                    e[g, MISC_GATE + br * A_HEADS + h, col:col + HEAD_DIM] = 1.0
    return jnp.asarray(e, BF16)


def _nsa(aq, misc, kc, vc, att, kw, vw, *, tq, pos0, n_cmp, n_slc):
    B, T, _ = aq.shape
    nch = kc.shape[2]
    Lk = att.shape[3]
    nsp = -(-max(n_slc, Lk // SLC_BLOCK) // LANES) * LANES
    m = _cmp_to_slc(nch, nsp, n_cmp, n_slc)
    tri = jnp.asarray(np.triu(np.ones((nsp, nsp)), 1), BF16)
    eg = _gate_expand()
    Lw = kw.shape[2]
    nt = T // tq
    in_specs = [
        pl.BlockSpec((None, tq, 256), lambda b, g, t: (b, t, g)),
        pl.BlockSpec((None, tq, LANES), lambda b, g, t: (b, t, 0)),
        _single((None, None, nch, LANES), lambda b, g, t: (b, g, 0, 0)),
        _single((None, None, nch, LANES), lambda b, g, t: (b, g, 0, 0)),
        _single((None, None, None, Lk, LANES), lambda b, g, t: (b, 0, g, 0, 0)),
        _single((None, None, None, Lk, LANES), lambda b, g, t: (b, 1, g, 0, 0)),
        _single((None, None, Lw, LANES), lambda b, g, t: (b, g, 0, 0)),
        _single((None, None, Lw, LANES), lambda b, g, t: (b, g, 0, 0)),
        _single(m.shape, lambda b, g, t: (0, 0)),
        _single(tri.shape, lambda b, g, t: (0, 0)),
        _single((None, LANES, 6 * LANES), lambda b, g, t: (g, 0, 0)),
    ]
    kern = functools.partial(_nsa_kernel, tq=tq, nt=nt, pos0=pos0, n_cmp=n_cmp, n_slc=n_slc)
    return pl.pallas_call(
        kern, grid=(B, A_GROUPS, nt), in_specs=in_specs,
        out_specs=pl.BlockSpec((None, tq, 256), lambda b, g, t: (b, t, g)),
        out_shape=jax.ShapeDtypeStruct((B, T, 512), F32),
        scratch_shapes=[pltpu.VMEM((nsp // LANES, tq, LANES), F32)],
        compiler_params=_cparams(("parallel", "parallel", "arbitrary")), name="nsa",
    )(aq, misc, kc, vc, att, att, kw, vw, m, tri, eg)


def _dsa_kernel(iq_ref, bq_ref, misc_ref, ik_ref, bk_ref, bv_ref, tri_ref, o_ref, key_ref, *, tq, pos0, k_sel):
    t0 = pos0 + pl.program_id(1) * tq
    pos = t0 + _row((tq, 1))
    nkt = t0 // KEY_TILE + 1
    reps = KEY_TILE // LANES
    qi4 = _heads_rows(iq_ref[...], 2)
    misc = misc_ref[...]
    wts = [misc[:, MISC_IW + h:MISC_IW + h + 1] for h in range(4)]
    lane_t = _lane((tq, KEY_TILE))

    def score_step(kt, _):
        off = pl.multiple_of(kt * KEY_TILE, KEY_TILE)
        d = jnp.maximum(_dot_nt(qi4, ik_ref[pl.ds(off, KEY_TILE), :]), 0.0)
        sc = d[0:tq] * wts[0] + d[tq:2 * tq] * wts[1] + d[2 * tq:3 * tq] * wts[2] + d[3 * tq:4 * tq] * wts[3]
        key_ref[kt] = _order_key(jnp.where(off + lane_t <= pos, sc, NEG_INF))
        return 0

    lax.fori_loop(0, nkt, score_step, 0)

    def count(pred, ref_vals):
        def body(kt, parts):
            key = key_ref[kt]
            out = []
            for c, val in zip(parts, ref_vals):
                for r in range(reps):
                    c = c + jnp.where(pred(key[:, r * LANES:(r + 1) * LANES], val), 1.0, 0.0)
                out.append(c)
            return tuple(out)
        parts = lax.fori_loop(0, nkt, body, tuple(jnp.zeros((tq, LANES), F32) for _ in ref_vals))
        return [jnp.broadcast_to(jnp.sum(p, axis=1, keepdims=True), (tq, LANES)) for p in parts]

    thr = _kth_largest(lambda cands: count(lambda k, w: k >= w, cands), float(k_sel), (tq, LANES))
    need = float(k_sel) - count(lambda k, w: k > w, [thr])[0]
    thr_w = jnp.concatenate([thr] * reps, axis=1)
    need_w = jnp.concatenate([need] * reps, axis=1)

    q8 = [_heads_rows(bq_ref[:, g * 256:(g + 1) * 256], 2) for g in range(B_KV_HEADS)]
    hg = B_HEADS // B_KV_HEADS
    r4 = hg * tq

    def step(kt, carry):
        ms, ls, accs, seen = carry
        off = pl.multiple_of(kt * KEY_TILE, KEY_TILE)
        key = key_ref[kt]
        eq = key == thr_w
        before = seen + _dot(jnp.where(eq, 1.0, 0.0).astype(BF16), tri_ref[...])
        sel = ((key > thr_w) | (eq & (before < need_w))) & (off + lane_t <= pos)
        bias = jnp.where(sel, 0.0, NEG_INF)[None]
        new_m, new_l, new_acc = [], [], []
        for g in range(B_KV_HEADS):
            sc = _dot_nt(q8[g], bk_ref[g, pl.ds(off, KEY_TILE), :]).reshape(hg, tq, KEY_TILE) + bias
            m_new = jnp.maximum(ms[g], jnp.max(sc, axis=-1, keepdims=True))
            alpha = jnp.exp(ms[g] - m_new)
            pe = jnp.exp(sc - m_new)
            new_m.append(m_new)
            new_l.append(alpha * ls[g] + jnp.sum(pe, axis=-1, keepdims=True))
            pv = _dot(pe.reshape(r4, KEY_TILE).astype(BF16), bv_ref[g, pl.ds(off, KEY_TILE), :])
            new_acc.append(alpha.reshape(r4, 1) * accs[g] + pv)
        seen = seen + jnp.sum(jnp.where(eq, 1.0, 0.0), axis=-1, keepdims=True)
        return tuple(new_m), tuple(new_l), tuple(new_acc), seen

    init = (tuple(jnp.full((hg, tq, 1), NEG_INF, F32) for _ in range(B_KV_HEADS)),
            tuple(jnp.zeros((hg, tq, 1), F32) for _ in range(B_KV_HEADS)),
            tuple(jnp.zeros((r4, LANES), F32) for _ in range(B_KV_HEADS)), jnp.zeros((tq, 1), F32))
    _, ls, accs, _ = lax.fori_loop(0, nkt, step, init)
    o = _pairs(jnp.concatenate([a / l.reshape(r4, 1) for a, l in zip(accs, ls)], axis=0), tq, 4)
    for c in range(4):
        o_ref[:, c * LANES:(c + 1) * LANES] = o[c]


def _dsa(iq, bq, misc, ika, batt, *, tq, pos0, k_sel):
    B, T, _ = bq.shape
    Lk = ika.shape[1]
    nt = T // tq
    tri = jnp.asarray(np.triu(np.ones((KEY_TILE, KEY_TILE)), 1), BF16)
    in_specs = [
        pl.BlockSpec((None, tq, 256), lambda b, t: (b, t, 0)),
        pl.BlockSpec((None, tq, 512), lambda b, t: (b, t, 0)),
        pl.BlockSpec((None, tq, LANES), lambda b, t: (b, t, 0)),
        _single((None, Lk, LANES), lambda b, t: (b, 0, 0)),
        _single((None, None, B_KV_HEADS, Lk, LANES), lambda b, t: (b, 0, 0, 0, 0)),
        _single((None, None, B_KV_HEADS, Lk, LANES), lambda b, t: (b, 1, 0, 0, 0)),
        _single(tri.shape, lambda b, t: (0, 0)),
    ]
    scratch = [pltpu.VMEM((Lk // KEY_TILE, tq, KEY_TILE), I32)]
    kern = functools.partial(_dsa_kernel, tq=tq, pos0=pos0, k_sel=k_sel)
    return pl.pallas_call(
        kern, grid=(B, nt), in_specs=in_specs,
        out_specs=pl.BlockSpec((None, tq, 512), lambda b, t: (b, t, 0)),
        out_shape=jax.ShapeDtypeStruct((B, T, 512), F32), scratch_shapes=scratch,
        compiler_params=_cparams(("parallel", "arbitrary")), name="dsa",
    )(iq, bq, misc, ika, batt, batt, tri)


GATHER_PAGES = 16


def _gather_kernel(pt_ref, cmpk_hbm, cmpv_hbm, slck_hbm, slcv_hbm, bk_hbm, bv_hbm, idx_hbm,
                   akv_ref, bkv_ref, misc_ref,
                   cmpk_ref, cmpv_ref, a_att_ref, b_att_ref, ika_ref,
                   b0, b1, b2, b3, b4, sem, *, nfull, ns):
    b, c = pl.program_id(0), pl.program_id(1)
    rows = GATHER_PAGES * PAGE_SIZE
    bufs = (b0, b1, b2, b3, b4)
    low = _lane((rows, LANES)) < HEAD_DIM

    def emit(xs):
        for i, (ref, s) in enumerate(((a_att_ref, 0), (a_att_ref, 1), (b_att_ref, 0), (b_att_ref, 1))):
            ref[s, 0] = jnp.where(low, xs[i], 0.0).astype(BF16)
            ref[s, 1] = jnp.where(low, pltpu.roll(xs[i], HEAD_DIM, 1), 0.0).astype(BF16)
        ika_ref[...] = jnp.where(low, xs[4], 0.0).astype(BF16)

    @pl.when(c < nfull)
    def _():
        copies = []
        for p in range(GATHER_PAGES):
            page = pt_ref[b, c * GATHER_PAGES + p]
            dst = pl.ds(p * PAGE_SIZE, PAGE_SIZE)
            copies.append(pltpu.make_async_copy(cmpk_hbm.at[page], cmpk_ref.at[0, dst], sem.at[0]))
            copies.append(pltpu.make_async_copy(cmpv_hbm.at[page], cmpv_ref.at[0, dst], sem.at[1]))
            for i, pool in enumerate((slck_hbm, slcv_hbm, bk_hbm, bv_hbm, idx_hbm)):
                copies.append(pltpu.make_async_copy(pool.at[page], bufs[i].at[dst], sem.at[2 + i]))
        for cp in copies:
            cp.start()
        for cp in copies:
            cp.wait()
        xi = b4[...]
        emit([b0[...], b1[...], b2[...], b3[...], jnp.concatenate([xi, jnp.zeros_like(xi)], axis=1)])

    @pl.when(c == nfull)
    def _():
        cmpk_ref[...] = jnp.zeros(cmpk_ref.shape, F32)
        cmpv_ref[...] = jnp.zeros(cmpv_ref.shape, F32)
        fill = jnp.zeros((rows - ns, LANES), F32)
        new = [akv_ref[2], akv_ref[3], bkv_ref[0], bkv_ref[1], misc_ref[...]]
        emit([jnp.concatenate([x, fill], axis=0) for x in new])


def _gather_pages(pools, page_table, kv, bkv, misc):
    DB, npg = page_table.shape
    ns = kv.shape[2]
    assert npg % GATHER_PAGES == 0
    nfull = npg // GATHER_PAGES
    rows = GATHER_PAGES * PAGE_SIZE
    L = (nfull + 1) * rows
    any_spec = pl.BlockSpec(memory_space=pl.ANY)
    new = lambda n: pl.BlockSpec((None, n, ns, LANES), lambda b, c, pt: (b, 0, 0, 0))
    raw = pl.BlockSpec((None, rows, LANES), lambda b, c, pt: (b, c, 0))
    dma_dst = pl.BlockSpec((1, rows, LANES), lambda b, c, pt: (b, c, 0))
    att = pl.BlockSpec((None, 2, A_GROUPS, rows, LANES), lambda b, c, pt: (b, 0, 0, c, 0))
    grid_spec = pltpu.PrefetchScalarGridSpec(
        num_scalar_prefetch=1, grid=(DB, nfull + 1),
        in_specs=[any_spec] * 7 + [new(6), new(2), pl.BlockSpec((None, ns, LANES), lambda b, c, pt: (b, 0, 0))],
        out_specs=[dma_dst, dma_dst, att, att, raw],
        scratch_shapes=[pltpu.VMEM((rows, LANES), F32)] * 4 + [pltpu.VMEM((rows, pools[6].shape[-1]), F32),
                                                                pltpu.SemaphoreType.DMA((7,))])
    out_shape = [jax.ShapeDtypeStruct((DB, L, LANES), F32), jax.ShapeDtypeStruct((DB, L, LANES), F32),
                 jax.ShapeDtypeStruct((DB, 2, A_GROUPS, L, LANES), BF16),
                 jax.ShapeDtypeStruct((DB, 2, B_KV_HEADS, L, LANES), BF16),
                 jax.ShapeDtypeStruct((DB, L, LANES), BF16)]
    return pl.pallas_call(
        functools.partial(_gather_kernel, nfull=nfull, ns=ns), grid_spec=grid_spec, out_shape=out_shape,
        compiler_params=_cparams(("parallel", "arbitrary")), name="gather_pages",
    )(page_table, *pools, kv, bkv, misc)


def _topk_rows(s, k, payload=None):
    n = s.shape[0]
    rows = _row(s.shape)
    vals, idxs = [], []
    for _ in range(k):
        m = jnp.max(s, axis=0, keepdims=True)
        idx = jnp.min(jnp.where(s == m, rows, n), axis=0, keepdims=True)
        hit = rows == idx
        vals.append(m)
        idxs.append(idx if payload is None else jnp.max(jnp.where(hit, payload, -1), axis=0, keepdims=True))
        s = jnp.where(hit, -jnp.inf, s)
    return jnp.concatenate(vals, axis=0), jnp.concatenate(idxs, axis=0)


def _route_kernel(x_ref, oa_ref, ob_ref, gates_ref, woa_ref, wob_ref, wo_ref, g2_ref, wq_ref, sub1_ref, sub2_ref,
                  h_ref, hn_ref, eid_ref, gate_ref):
    gates = gates_ref[...]
    ya = _dot(oa_ref[...].astype(BF16), woa_ref[...])
    yb = _dot(ob_ref[...].astype(BF16), wob_ref[...])
    mix = gates[:, :D_MODEL] * ya + gates[:, D_MODEL:] * yb
    h = x_ref[...] + _dot(mix.astype(BF16), wo_ref[...])
    h_ref[...] = h
    hn = h * lax.rsqrt(jnp.mean(h * h, axis=-1, keepdims=True) + EPS) * g2_ref[...]
    hn_ref[...] = hn
    q = _dot(hn.astype(BF16), wq_ref[...])
    half = PEER_KEY_DIM // 2
    for hd in range(PEER_HEADS):
        q1 = q[:, hd * PEER_KEY_DIM:hd * PEER_KEY_DIM + half].astype(BF16)
        q2 = q[:, hd * PEER_KEY_DIM + half:(hd + 1) * PEER_KEY_DIM].astype(BF16)
        v1, i1 = _topk_rows(_dot_nt(sub1_ref[hd], q1), PEER_TOPK)
        v2, i2 = _topk_rows(_dot_nt(sub2_ref[hd], q2), PEER_TOPK)
        cand = jnp.concatenate([v1[r:r + 1] + v2 for r in range(PEER_TOPK)], axis=0)
        cid = jnp.concatenate([i1[r:r + 1] * PEER_NKEYS + i2 for r in range(PEER_TOPK)], axis=0)
        vt, et = _topk_rows(cand, PEER_TOPK, payload=cid)
        e = jnp.exp(vt - vt[0:1])
        eid_ref[0, hd * PEER_TOPK:(hd + 1) * PEER_TOPK, :] = et
        gate_ref[0, hd * PEER_TOPK:(hd + 1) * PEER_TOPK, :] = e / jnp.sum(e, axis=0, keepdims=True)


def _route(x, oa, ob, gates, woa, wob, wo, g2, wq, sub1, sub2, tm):
    N, D = x.shape
    nt = N // tm
    tok = lambda c: pl.BlockSpec((tm, c), lambda i: (i, 0))
    const = lambda a: pl.BlockSpec(a.shape, lambda i: (0,) * a.ndim)
    slot = pl.BlockSpec((1, PEER_HEADS * PEER_TOPK, tm), lambda i: (i, 0, 0))
    consts = (woa, wob, wo, g2, wq, sub1, sub2)
    return pl.pallas_call(
        _route_kernel, grid=(nt,),
        in_specs=[tok(D), tok(512), tok(512), tok(2 * D)] + [const(a) for a in consts],
        out_specs=(tok(D), tok(D), slot, slot),
        out_shape=(jax.ShapeDtypeStruct((N, D), F32), jax.ShapeDtypeStruct((N, D), F32),
                   jax.ShapeDtypeStruct((nt, PEER_HEADS * PEER_TOPK, tm), I32),
                   jax.ShapeDtypeStruct((nt, PEER_HEADS * PEER_TOPK, tm), F32)),
        compiler_params=_cparams(("parallel",)), name="route",
    )(x, oa, ob, gates, *consts)


def _pack_table(t):
    E, D = t.shape
    b = lax.bitcast_convert_type(t.astype(BF16), jnp.uint16).astype(jnp.uint32)
    w = b[:, :D // 2] | (b[:, D // 2:] << 16)
    return lax.bitcast_convert_type(w, I32).reshape(E, D // 256, LANES)


def _unpack_row(w):
    return pltpu.bitcast(w << 16, F32), pltpu.bitcast(w & jnp.int32(-65536), F32)


def _load_table(tab_hbm, tab_vmem, sem):
    @pl.when(pl.program_id(0) == 0)
    def _():
        cp = pltpu.make_async_copy(tab_hbm, tab_vmem, sem)
        cp.start()
        cp.wait()


PEER_LANE_GROUP = 8
PEER_USES_PER_ROW = LANES // PEER_LANE_GROUP


def _peer_act_kernel(eid_ref, x_ref, gsum_ref, tab_hbm, act_ref, tab_vmem, sem, stage0_ref, stage1_ref, v_ref,
                     *, tt, n_slot):
    _load_table(tab_hbm, tab_vmem, sem)
    rows_per_use = stage0_ref.shape[0] // n_slot
    lane, sub = _lane((PEER_LANE_GROUP, LANES)), _row((PEER_LANE_GROUP, LANES))
    masks = [((lane >> 3) == c) & ((lane & 7) == sub) for c in range(PEER_USES_PER_ROW)]
    groups = n_slot // PEER_USES_PER_ROW
    vrows = groups * PEER_LANE_GROUP

    def one(t, stage_ref):
        base = t * n_slot
        for j in range(n_slot):
            stage_ref[rows_per_use * j:rows_per_use * (j + 1), :] = tab_vmem[eid_ref[base + j]]
        bm = pltpu.bitcast(stage_ref[...], BF16)
        xrep = jnp.concatenate([x_ref[t]] * PEER_USES_PER_ROW, axis=0).astype(BF16)
        g = _dot_nt(bm, xrep)
        vs = []
        for m in range(groups):
            v = jnp.zeros((PEER_LANE_GROUP, LANES), F32)
            for c in range(PEER_USES_PER_ROW):
                r0 = PEER_LANE_GROUP * (PEER_USES_PER_ROW * m + c)
                v = v + jnp.where(masks[c], g[r0:r0 + PEER_LANE_GROUP], 0.0)
            vs.append(v)
        v_ref[pl.ds(pl.multiple_of(t * vrows, vrows), vrows), :] = jnp.concatenate(vs, axis=0)

    def pair(i, _):
        one(2 * i, stage0_ref)
        one(2 * i + 1, stage1_ref)
        return 0

    lax.fori_loop(0, tt // 2, pair, 0)
    a = _split_dot(v_ref[...], gsum_ref[...])
    act_ref[...] = jnp.sum(a.reshape(tt * groups, PEER_LANE_GROUP, LANES), axis=1).reshape(tt, groups, LANES)


def _gelu_gate_kernel(act_ref, gate_ref, w_ref):
    act = act_ref[...]
    w_ref[...] = gate_ref[...] * (act * (lax.erf(act * (2.0 ** -0.5)) + 1.0) * 0.5)


def _peer_out_kernel(eid_ref, w_ref, h_ref, tab_hbm, y_ref, tab_vmem, sem, *, tt, n_slot):
    _load_table(tab_hbm, tab_vmem, sem)
    half = h_ref.shape[1] // 2
    chunk = 16

    def token(t, _):
        def part(c, acc):
            acc = list(acc)
            base = t * n_slot + c * chunk
            for j in range(chunk):
                lo, hi = _unpack_row(tab_vmem[eid_ref[base + j]])
                wj = w_ref[base + j]
                acc[2 * (j % 2)] = acc[2 * (j % 2)] + wj * lo
                acc[2 * (j % 2) + 1] = acc[2 * (j % 2) + 1] + wj * hi
            return tuple(acc)

        acc = lax.fori_loop(0, n_slot // chunk, part, tuple(jnp.zeros((half, LANES), F32) for _ in range(4)))
        y_ref[t] = h_ref[t] + jnp.concatenate([acc[0] + acc[2], acc[1] + acc[3]], axis=0)
        return 0

    lax.fori_loop(0, tt, token, 0)


def _peer(eid, gate, hn, h, u_tab, v_tab, tt):
    nt, n_slot, _ = eid.shape
    N, D = h.shape
    sub = D // LANES
    assert n_slot == LANES and sub == PEER_LANE_GROUP
    eid_flat = eid.transpose(0, 2, 1).reshape(-1)
    gate_t = gate.transpose(0, 2, 1).reshape(N, n_slot)
    xr = hn.reshape(N, 2, sub // 2, LANES).transpose(0, 2, 1, 3).reshape(N, sub, LANES)
    gsum = jnp.asarray((np.arange(LANES)[:, None] // PEER_LANE_GROUP == np.arange(LANES)[None, :]), BF16)
    smem = lambda: pl.BlockSpec((n_slot * tt,), lambda i: (i,), memory_space=pltpu.SMEM)
    rows = pl.BlockSpec((tt, sub, LANES), lambda i: (i, 0, 0))
    any_spec = pl.BlockSpec(memory_space=pl.ANY)
    tab_scratch = [pltpu.VMEM(u_tab.shape, I32), pltpu.SemaphoreType.DMA(())]
    act = pl.pallas_call(
        functools.partial(_peer_act_kernel, tt=tt, n_slot=n_slot), grid=(nt,),
        in_specs=[smem(), rows, pl.BlockSpec((LANES, LANES), lambda i: (0, 0)), any_spec], out_specs=rows,
        out_shape=jax.ShapeDtypeStruct((N, sub, LANES), F32),
        scratch_shapes=tab_scratch + [pltpu.VMEM((n_slot * u_tab.shape[1], LANES), I32)] * 2
        + [pltpu.VMEM((tt * n_slot // PEER_USES_PER_ROW * PEER_LANE_GROUP, LANES), F32)],
        compiler_params=_cparams(("arbitrary",)), name="peer_act",
    )(eid_flat, xr, gsum, u_tab)
    act_t = act[:, :, :PEER_USES_PER_ROW].reshape(N, n_slot)
    tw = 1024 if N % 1024 == 0 else N
    tile = pl.BlockSpec((tw, n_slot), lambda i: (i, 0))
    w = pl.pallas_call(
        _gelu_gate_kernel, grid=(N // tw,), in_specs=[tile, tile], out_specs=tile,
        out_shape=jax.ShapeDtypeStruct((N, n_slot), F32),
        compiler_params=_cparams(("parallel",)), name="gelu_gate",
    )(act_t, gate_t)
    y = pl.pallas_call(
        functools.partial(_peer_out_kernel, tt=tt, n_slot=n_slot), grid=(nt,),
        in_specs=[smem(), smem(), rows, any_spec], out_specs=rows,
        out_shape=jax.ShapeDtypeStruct((N, sub, LANES), F32), scratch_shapes=tab_scratch,
        compiler_params=_cparams(("arbitrary",)), name="peer_out",
    )(eid_flat, w.reshape(-1), h.reshape(N, sub, LANES), v_tab)
    return y.reshape(N, D)


def _relayout_w_in(w_in):
    names = (('a_q', 512), ('a_k_cmp', 128), ('a_v_cmp', 128), ('a_k_slc', 128), ('a_v_slc', 128),
             ('a_k_win', 128), ('a_v_win', 128), ('a_gate', 24), ('b_q', 512), ('b_k', 128), ('b_v', 128),
             ('i_q', 256), ('i_w', 4), ('i_k', 64), ('merge', 2 * D_MODEL))
    sec, o = {}, 0
    for name, n in names:
        sec[name] = w_in[:, o:o + n]
        o += n
    pad = jnp.zeros((w_in.shape[0], LANES - 64 - 4 - 24), w_in.dtype)
    order = ['a_q', 'a_k_cmp', 'a_v_cmp', 'a_k_slc', 'a_v_slc', 'a_k_win', 'a_v_win', 'b_q', 'b_k', 'b_v',
             'i_q', 'i_k', 'i_w', 'a_gate']
    return jnp.concatenate([sec[n] for n in order] + [pad, sec['merge']], axis=1).astype(BF16)


def _group_lanes(x):
    z = jnp.zeros_like(x[..., :HEAD_DIM])
    parts = [jnp.concatenate([x[..., g * HEAD_DIM:(g + 1) * HEAD_DIM], z], axis=-1) for g in range(2)]
    return jnp.stack(parts, axis=-3).astype(BF16)


def _pad_rows(x, total, axis):
    pad = [(0, 0)] * x.ndim
    pad[axis] = (0, total - x.shape[axis])
    return jnp.pad(x, pad)


def kernel(x_prompt, x_sample, cache_a_cmp_k, cache_a_cmp_v, cache_a_slc_k, cache_a_slc_v, state_a_win_k, state_a_win_v, cache_b_k, cache_b_v, cache_b_idx_k, page_table, norm1_g, w_in, g_qa, g_ka_cmp, g_ka_slc, g_ka_win, g_qb, g_kb, g_kidx, w_cmp_k, pe_cmp_k, w_cmp_v, pe_cmp_v, g_kc, w_out_a, w_out_b, w_o, norm2_g, w_peer_q, peer_subkeys, peer_u, peer_v):
    B, S, D = x_prompt.shape
    DB, NS, _ = x_sample.shape
    npg = page_table.shape[1]
    past = npg * PAGE_SIZE
    assert D == D_MODEL and S % KEY_TILE == 0 and past % KEY_TILE == 0 and NS < CMP_STRIDE
    assert state_a_win_k.shape[1] == WINDOW

    wp = _relayout_w_in(w_in)
    ones = jnp.ones((HEAD_DIM,), F32)
    gains = jnp.stack([jnp.tile(g, 2) for g in (g_qa, g_ka_cmp, g_ka_slc, g_ka_win, g_qb, g_kb)]
                      + [jnp.concatenate([g_kidx, ones]), jnp.tile(ones, 2)])
    wk2, pek2 = _compress_weights(w_cmp_k, pe_cmp_k)
    wv2, pev2 = _compress_weights(w_cmp_v, pe_cmp_v)
    woa, wob, wo = w_out_a.astype(BF16), w_out_b.astype(BF16), w_o.astype(BF16)
    wq = w_peer_q.reshape(D, PEER_HEADS * PEER_KEY_DIM).astype(BF16)
    sub1, sub2 = peer_subkeys[0].astype(BF16), peer_subkeys[1].astype(BF16)
    g2 = norm2_g.reshape(1, D)
    u_tab, v_tab = _pack_table(peer_u), _pack_table(peer_v)

    def finish(x, oa, ob, gates):
        n = x.shape[0] * x.shape[1]
        tt = 128 if n % 128 == 0 else n
        h, hn, eid, gate = _route(x.reshape(n, D), oa.reshape(n, 512), ob.reshape(n, 512),
                                  gates.reshape(n, 2 * D), woa, wob, wo, g2, wq, sub1, sub2, tt)
        return _peer(eid, gate, hn, h, u_tab, v_tab, tt).reshape(x.shape)

    def heads(a, n):
        return a.reshape(a.shape[0], a.shape[1], n, HEAD_DIM)

    total = past + NS
    aq, bq, iq, kv, bkv, misc, ika, gates, a_att, b_att = _project(
        x_sample, past + jnp.arange(NS), norm1_g, wp, gains, NS)
    flat = lambda c: c.reshape(c.shape[0], PAGE_SIZE, -1)
    cmpk_s, cmpv_s, a_att_s, b_att_s, ika_s = _gather_pages(
        [flat(c) for c in (cache_a_cmp_k, cache_a_cmp_v, cache_a_slc_k, cache_a_slc_v, cache_b_k, cache_b_v,
                           cache_b_idx_k)], page_table, kv, bkv, misc)
    nch_s = total // CMP_STRIDE
    assert nch_s * CMP_STRIDE <= past
    kc, vc = _compress((cmpk_s, 0), (cmpv_s, 0), nch_s, wk2, wv2, pek2, pev2, g_kc)
    wl = -(-(WINDOW + NS) // LANES) * LANES
    kw_s = _pad_rows(jnp.concatenate([_group_lanes(state_a_win_k.reshape(DB, WINDOW, LANES)), a_att[:, 2]], axis=2), wl, 2)
    vw_s = _pad_rows(jnp.concatenate([_group_lanes(state_a_win_v.reshape(DB, WINDOW, LANES)), a_att[:, 3]], axis=2), wl, 2)
    o_a = _nsa(aq, misc, kc, vc, a_att_s, kw_s, vw_s, tq=NS, pos0=past, n_cmp=nch_s - 1,
               n_slc=-(-total // SLC_BLOCK))
    o_b = _dsa(iq, bq, misc, ika_s, b_att_s, tq=NS, pos0=past, k_sel=min(DSA_TOPK_MAX, total // 4))
    y_sample = finish(x_sample, o_a, o_b, gates)
    s_out = [heads(kv[:, i], A_GROUPS) for i in range(4)]
    s_out += [jnp.concatenate([state_a_win_k[:, NS:], heads(kv[:, 4], A_GROUPS)], axis=1),
              jnp.concatenate([state_a_win_v[:, NS:], heads(kv[:, 5], A_GROUPS)], axis=1)]
    s_out += [heads(bkv[:, 0], B_KV_HEADS), heads(bkv[:, 1], B_KV_HEADS), misc[:, :, :IDX_DIM]]

    tq = 128
    aq, bq, iq, kv, bkv, misc, ika, gates, a_att, b_att = _project(
        x_prompt, jnp.arange(S), norm1_g, wp, gains, 256)
    nch = S // CMP_STRIDE
    kc, vc = _compress((kv, 0), (kv, 1), nch, wk2, wv2, pek2, pev2, g_kc)
    front = ((0, 0), (0, 0), (WINDOW, 0), (0, 0))
    o_a = _nsa(aq, misc, kc, vc, a_att, jnp.pad(a_att[:, 2], front), jnp.pad(a_att[:, 3], front),
               tq=tq, pos0=0, n_cmp=nch - 1, n_slc=-(-S // SLC_BLOCK))
    o_b = _dsa(iq, bq, misc, ika, b_att, tq=tq, pos0=0, k_sel=min(DSA_TOPK_MAX, S // 4))
    y_prompt = finish(x_prompt, o_a, o_b, gates)
    win_p = min(WINDOW, S)
    p_out = [heads(kv[:, i], A_GROUPS) for i in range(4)]
    p_out += [heads(kv[:, i, S - win_p:], A_GROUPS) for i in (4, 5)]
    p_out += [heads(bkv[:, 0], B_KV_HEADS), heads(bkv[:, 1], B_KV_HEADS), misc[:, :, :IDX_DIM]]
    return (y_prompt, y_sample, *p_out, *s_out)
```
